```python
import math
import jax
import jax.numpy as jnp
from jax import lax
import numpy as np

D_MODEL = 1024
BATCH = 2
SEQ = 8192
DEPTH = 2

MEM_LEN = 256
NORM_EPS = 1e-6
SWA_Q_HEADS = 8
SWA_KV_HEADS = 2
SWA_HEAD_DIM = 64
SWA_WINDOW = 128
SWA_BLOCK = 128
ROPE_THETA = 150000.0
MOBA_HEADS = 8
MOBA_HEAD_DIM = 64
MOBA_BLOCK = 256
MOBA_TOPK = 3
MOBA_Q_CHUNK = 64
SSD_D_INNER = D_MODEL
SSD_HEAD_DIM = 64
SSD_HEADS = SSD_D_INNER // SSD_HEAD_DIM
SSD_GROUPS = 2
SSD_D_STATE = 128
SSD_CONV = 4
SSD_CHUNK = 128
DT_MIN = 0.001
DT_MAX = 0.1
X_HEADS = 4
X_HEAD_DIM = D_MODEL // X_HEADS
D_FF = 4 * D_MODEL
N_BRANCH = 3

SWA_Q_W = SWA_Q_HEADS * SWA_HEAD_DIM
SWA_KV_W = SWA_KV_HEADS * SWA_HEAD_DIM
MOBA_W = MOBA_HEADS * MOBA_HEAD_DIM
SSD_BC_W = SSD_GROUPS * SSD_D_STATE
SSD_XBC_W = SSD_D_INNER + 2 * SSD_BC_W
IN_SIZES = (SWA_Q_W, SWA_KV_W, SWA_KV_W, MOBA_W, MOBA_W, MOBA_W, SSD_D_INNER, SSD_XBC_W, SSD_HEADS)
D_IN_PROJ = sum(IN_SIZES)

kernel_name = 'hybrid_swa_moba_ssd_gated_block'


def split_cols(t, sizes):
    cuts, acc = [], 0
    for s in sizes[:-1]:
        acc += s
        cuts.append(acc)
    return jnp.split(t, cuts, axis=-1)


def rmsnorm(x, w):
    xf = x.astype(jnp.float32)
    y = xf * lax.rsqrt(jnp.mean(xf * xf, axis=-1, keepdims=True) + NORM_EPS)
    return (y * w.astype(jnp.float32)).astype(x.dtype)


def apply_rope(t, positions):
    half = t.shape[-1] // 2
    inv = ROPE_THETA ** (-jnp.arange(half, dtype=jnp.float32) / half)
    ang = positions.astype(jnp.float32)[..., None] * inv
    cos = jnp.cos(ang)[:, :, None, :]
    sin = jnp.sin(ang)[:, :, None, :]
    tf = t.astype(jnp.float32)
    t1, t2 = tf[..., :half], tf[..., half:]
    return jnp.concatenate([t1 * cos - t2 * sin, t2 * cos + t1 * sin], axis=-1).astype(t.dtype)


def swa_sink_attention(q, k, v, sinks):
    B, S, _, hd = q.shape
    nb = S // SWA_BLOCK
    G = SWA_Q_HEADS // SWA_KV_HEADS
    qb = q.reshape(B, nb, SWA_BLOCK, SWA_KV_HEADS, G, hd)

    def bands(t):
        tp = jnp.pad(t, ((0, 0), (SWA_BLOCK, 0), (0, 0), (0, 0)))
        tp = tp.reshape(B, nb + 1, SWA_BLOCK, SWA_KV_HEADS, hd)
        return jnp.concatenate([tp[:, :-1], tp[:, 1:]], axis=2)

    kb, vb = bands(k), bands(v)
    s = jnp.einsum('bnqkgd,bnskd->bnkgqs', qb, kb).astype(jnp.float32) * (hd ** -0.5)
    qi = jnp.arange(SWA_BLOCK)[:, None] + SWA_BLOCK
    si = jnp.arange(2 * SWA_BLOCK)[None, :]
    delta = qi - si
    kpos = jnp.arange(nb)[:, None, None] * SWA_BLOCK - SWA_BLOCK + si[None]
    mask = (delta >= 0)[None] & (delta < SWA_WINDOW)[None] & (kpos >= 0)
    s = jnp.where(mask[None, :, None, None], s, -jnp.inf)
    sink = sinks.astype(jnp.float32).reshape(SWA_KV_HEADS, G)[None, None, :, :, None, None]
    m = jnp.maximum(jnp.max(s, axis=-1, keepdims=True), sink)
    p = jnp.exp(s - m)
    denom = jnp.sum(p, axis=-1, keepdims=True) + jnp.exp(sink - m)
    o = jnp.einsum('bnkgqs,bnskd->bnqkgd', (p / denom).astype(v.dtype), vb)
    return o.reshape(B, S, SWA_Q_HEADS * hd)


def moba_attention(q, k, v):
    B, S, H, hd = q.shape
    nkb = -(-S // MOBA_BLOCK)
    pad = nkb * MOBA_BLOCK - S
    scale = hd ** -0.5
    qh = jnp.transpose(q, (0, 2, 1, 3))

    def blocks(t):
        t = jnp.pad(jnp.transpose(t, (0, 2, 1, 3)), ((0, 0), (0, 0), (0, pad), (0, 0)))
        return t.reshape(B, H, nkb, MOBA_BLOCK, hd)

    kb, vb = blocks(k), blocks(v)
    k_mean = jnp.mean(kb.astype(jnp.float32), axis=3)
    topk = min(MOBA_TOPK, nkb)
    b_idx = jnp.arange(B)[:, None, None, None]
    h_idx = jnp.arange(H)[None, :, None, None]
    blk_ids = jnp.arange(nkb)
    n_sel = topk * MOBA_BLOCK

    def chunk(c):
        start = c * MOBA_Q_CHUNK
        qs = lax.dynamic_slice_in_dim(qh, start, MOBA_Q_CHUNK, axis=2)
        qpos = start + jnp.arange(MOBA_Q_CHUNK)
        own = start // MOBA_BLOCK
        gate = jnp.einsum('bhqd,bhnd->bhqn', qs.astype(jnp.float32), k_mean)
        gate = jnp.where(blk_ids < own, gate, -jnp.inf)
        _, sel = lax.top_k(gate, topk)
        sel_ok = sel < own
        kg = kb[b_idx, h_idx, sel]
        vg = vb[b_idx, h_idx, sel]
        s_sel = jnp.einsum('bhqd,bhqksd->bhqks', qs, kg).astype(jnp.float32) * scale
        s_sel = jnp.where(sel_ok[..., None], s_sel, -jnp.inf).reshape(B, H, MOBA_Q_CHUNK, n_sel)
        k_own = lax.dynamic_index_in_dim(kb, own, axis=2, keepdims=False)
        v_own = lax.dynamic_index_in_dim(vb, own, axis=2, keepdims=False)
        s_own = jnp.einsum('bhqd,bhsd->bhqs', qs, k_own).astype(jnp.float32) * scale
        kpos = own * MOBA_BLOCK + jnp.arange(MOBA_BLOCK)
        s_own = jnp.where(kpos[None, :] <= qpos[:, None], s_own, -jnp.inf)
        p = jax.nn.softmax(jnp.concatenate([s_sel, s_own], axis=-1), axis=-1).astype(v.dtype)
        p_sel = p[..., :n_sel].reshape(B, H, MOBA_Q_CHUNK, topk, MOBA_BLOCK)
        p_own = p[..., n_sel:]
        return (jnp.einsum('bhqks,bhqksd->bhqd', p_sel, vg)
                + jnp.einsum('bhqs,bhsd->bhqd', p_own, v_own))

    out = lax.map(chunk, jnp.arange(S // MOBA_Q_CHUNK))
    return jnp.transpose(out, (1, 0, 3, 2, 4)).reshape(B, S, H * hd)


def causal_depthwise_conv(t, w, b):
    y = lax.conv_general_dilated(t, w[:, None, :], window_strides=(1,), padding=[(SSD_CONV - 1, 0)],
                                 dimension_numbers=('NWC', 'WIO', 'NWC'), feature_group_count=t.shape[-1])
    return y + b


def segsum_exp(a_cs):
    L = a_cs.shape[-1]
    diff = a_cs[..., :, None] - a_cs[..., None, :]
    mask = jnp.tril(jnp.ones((L, L), dtype=bool))
    return jnp.exp(jnp.where(mask, diff, -jnp.inf))


def ssd_scan(xh, dt, a, bm, cm):
    B, S, H, P = xh.shape
    G, N = bm.shape[2], bm.shape[3]
    R = H // G
    L = SSD_CHUNK
    nc = S // L
    x_ = (xh.astype(jnp.float32) * dt[..., None]).reshape(B, nc, L, G, R, P)
    a_ = jnp.transpose((dt * a).reshape(B, nc, L, G, R), (0, 3, 4, 1, 2))
    a_cs = jnp.cumsum(a_, axis=-1)
    b_ = bm.astype(jnp.float32).reshape(B, nc, L, G, N)
    c_ = cm.astype(jnp.float32).reshape(B, nc, L, G, N)
    cb = jnp.einsum('bclgn,bcsgn->bgcls', c_, b_)
    wmat = cb[:, :, None] * segsum_exp(a_cs)
    y_diag = jnp.einsum('bgrcls,bcsgrp->bclgrp', wmat, x_)
    decay_states = jnp.exp(a_cs[..., -1:] - a_cs)
    states = jnp.einsum('bclgn,bgrcl,bclgrp->bcgrpn', b_, decay_states, x_)
    chunk_decay = jnp.exp(a_cs[..., -1])

    def step(h, inp):
        st, dec = inp
        return h * dec[..., None, None] + st, h

    h0 = jnp.zeros((B, G, R, P, N), jnp.float32)
    _, prev = lax.scan(step, h0, (jnp.moveaxis(states, 1, 0), jnp.moveaxis(chunk_decay, 3, 0)))
    y_off = jnp.einsum('bclgn,cbgrpn,bgrcl->bclgrp', c_, prev, jnp.exp(a_cs))
    return (y_diag + y_off).reshape(B, S, H, P)


def ssd_mixer(z, xbc, dt_raw, conv_w, conv_b, dt_bias, a_log, d_skip, norm_w):
    B, S, _ = z.shape
    xbc = jax.nn.silu(causal_depthwise_conv(xbc, conv_w, conv_b))
    xs, bm, cm = split_cols(xbc, (SSD_D_INNER, SSD_BC_W, SSD_BC_W))
    xh = xs.reshape(B, S, SSD_HEADS, SSD_HEAD_DIM)
    bm = bm.reshape(B, S, SSD_GROUPS, SSD_D_STATE)
    cm = cm.reshape(B, S, SSD_GROUPS, SSD_D_STATE)
    dt = jax.nn.softplus(dt_raw.astype(jnp.float32) + dt_bias.astype(jnp.float32))
    a = -jnp.exp(a_log.astype(jnp.float32))
    y = ssd_scan(xh, dt, a, bm, cm) + xh.astype(jnp.float32) * d_skip.astype(jnp.float32)[:, None]
    y = y.reshape(B, S, SSD_D_INNER) * jax.nn.silu(z.astype(jnp.float32))
    yg = y.reshape(B, S, SSD_GROUPS, SSD_D_INNER // SSD_GROUPS)
    yg = yg * lax.rsqrt(jnp.mean(yg * yg, axis=-1, keepdims=True) + NORM_EPS)
    return (yg.reshape(B, S, SSD_D_INNER) * norm_w.astype(jnp.float32)).astype(z.dtype)


def hybrid_layer(x, mem, positions, norm_mix, w_in, w_gate, b_gate, swa_sinks, conv_w, conv_b,
                 dt_bias, a_log, d_skip, ssd_norm, w_br_swa, w_br_moba, w_br_ssd, w_mix_out,
                 norm_cross, norm_mem, w_cq, w_ckv, w_co, norm_mlp, w_up, w_down):
    B, S, D = x.shape
    h = rmsnorm(x, norm_mix)
    q_a, k_a, v_a, q_b, k_b, v_b, z, xbc, dt_raw = split_cols(h @ w_in, IN_SIZES)
    q_a = apply_rope(q_a.reshape(B, S, SWA_Q_HEADS, SWA_HEAD_DIM), positions)
    k_a = apply_rope(k_a.reshape(B, S, SWA_KV_HEADS, SWA_HEAD_DIM), positions)
    v_a = v_a.reshape(B, S, SWA_KV_HEADS, SWA_HEAD_DIM)
    y_a = swa_sink_attention(q_a, k_a, v_a, swa_sinks)
    y_b = moba_attention(q_b.reshape(B, S, MOBA_HEADS, MOBA_HEAD_DIM),
                         k_b.reshape(B, S, MOBA_HEADS, MOBA_HEAD_DIM),
                         v_b.reshape(B, S, MOBA_HEADS, MOBA_HEAD_DIM))
    y_c = ssd_mixer(z, xbc, dt_raw, conv_w, conv_b, dt_bias, a_log, d_skip, ssd_norm)
    gates = jax.nn.sigmoid((h @ w_gate + b_gate).astype(jnp.float32)).astype(x.dtype)
    g_a, g_b, g_c = jnp.split(gates, N_BRANCH, axis=-1)
    merged = g_a * (y_a @ w_br_swa) + g_b * (y_b @ w_br_moba) + g_c * (y_c @ w_br_ssd)
    x = x + merged @ w_mix_out
    hq = rmsnorm(x, norm_cross)
    hm = rmsnorm(mem, norm_mem)
    q = (hq @ w_cq).reshape(B, S, X_HEADS, X_HEAD_DIM)
    k, v = jnp.split(hm @ w_ckv, 2, axis=-1)
    k = k.reshape(B, MEM_LEN, X_HEADS, X_HEAD_DIM)
    v = v.reshape(B, MEM_LEN, X_HEADS, X_HEAD_DIM)
    s = jnp.einsum('bshd,bmhd->bhsm', q, k).astype(jnp.float32) * (X_HEAD_DIM ** -0.5)
    p = jax.nn.softmax(s, axis=-1).astype(v.dtype)
    o = jnp.einsum('bhsm,bmhd->bshd', p, v).reshape(B, S, D)
    x = x + o @ w_co
    hf = rmsnorm(x, norm_mlp)
    return x + jnp.square(jax.nn.relu(hf @ w_up)) @ w_down


def setup_inputs(seed: int = 0) -> dict:
    key = jax.random.key(seed)
    ks = jax.random.split(key, 32)
    f32 = jnp.float32
    D = D_MODEL

    def dense(k, shape, fan_in):
        return jax.random.normal(k, shape, f32) * (fan_in ** -0.5)

    def gain(k, shape):
        return 1.0 + 0.05 * jax.random.normal(k, shape, f32)

    def small(k, shape):
        return 0.01 * jax.random.normal(k, shape, f32)

    dt0 = jnp.exp(jax.random.uniform(ks[9], (DEPTH, SSD_HEADS), f32, math.log(DT_MIN), math.log(DT_MAX)))
    return {
        'x': jax.random.normal(ks[0], (BATCH, SEQ, D), f32),
        'mem': jax.random.normal(ks[1], (BATCH, MEM_LEN, D), f32),
        'positions': jnp.broadcast_to(jnp.arange(SEQ, dtype=jnp.int32)[None, :], (BATCH, SEQ)),
        'norm_mix': gain(ks[2], (DEPTH, D)),
        'w_in': dense(ks[3], (DEPTH, D, D_IN_PROJ), D),
        'w_gate': dense(ks[4], (DEPTH, D, N_BRANCH * D), D),
        'b_gate': small(ks[5], (DEPTH, N_BRANCH * D)),
        'swa_sinks': jax.random.normal(ks[6], (DEPTH, SWA_Q_HEADS), f32),
        'conv_w': dense(ks[7], (DEPTH, SSD_CONV, SSD_XBC_W), SSD_CONV),
        'conv_b': small(ks[8], (DEPTH, SSD_XBC_W)),
        'dt_bias': dt0 + jnp.log(-jnp.expm1(-dt0)),
        'a_log': jnp.log(jax.random.uniform(ks[10], (DEPTH, SSD_HEADS), f32, 1.0, 16.0)),
        'd_skip': gain(ks[11], (DEPTH, SSD_HEADS)),
        'ssd_norm': gain(ks[12], (DEPTH, SSD_D_INNER)),
        'w_br_swa': dense(ks[13], (DEPTH, SWA_Q_W, D), SWA_Q_W),
        'w_br_moba': dense(ks[14], (DEPTH, MOBA_W, D), MOBA_W),
        'w_br_ssd': dense(ks[15], (DEPTH, SSD_D_INNER, D), SSD_D_INNER),
        'w_mix_out': dense(ks[16], (DEPTH, D, D), D),
        'norm_cross': gain(ks[17], (DEPTH, D)),
        'norm_mem': gain(ks[18], (DEPTH, D)),
        'w_cq': dense(ks[19], (DEPTH, D, D), D),
        'w_ckv': dense(ks[20], (DEPTH, D, 2 * D), D),
        'w_co': dense(ks[21], (DEPTH, D, D), D),
        'norm_mlp': gain(ks[22], (DEPTH, D)),
        'w_up': dense(ks[23], (DEPTH, D, D_FF), D),
        'w_down': dense(ks[24], (DEPTH, D_FF, D), D_FF),
        'final_norm': gain(ks[25], (D,)),
    }


def reference(x, mem, positions, norm_mix, w_in, w_gate, b_gate, swa_sinks, conv_w, conv_b,
              dt_bias, a_log, d_skip, ssd_norm, w_br_swa, w_br_moba, w_br_ssd, w_mix_out,
              norm_cross, norm_mem, w_cq, w_ckv, w_co, norm_mlp, w_up, w_down, final_norm):
    for l in range(DEPTH):
        x = hybrid_layer(x, mem, positions, norm_mix[l], w_in[l], w_gate[l], b_gate[l], swa_sinks[l],
                         conv_w[l], conv_b[l], dt_bias[l], a_log[l], d_skip[l], ssd_norm[l],
                         w_br_swa[l], w_br_moba[l], w_br_ssd[l], w_mix_out[l], norm_cross[l],
                         norm_mem[l], w_cq[l], w_ckv[l], w_co[l], norm_mlp[l], w_up[l], w_down[l])
    return rmsnorm(x, final_norm)
```

```python
import functools
import math

import jax
import jax.numpy as jnp
from jax import lax
from jax.experimental import pallas as pl
from jax.experimental.pallas import tpu as pltpu

F32 = jnp.float32
BF16 = jnp.bfloat16

D_MODEL = 1024
MEM_LEN = 256
NORM_EPS = 1e-6
SWA_Q_HEADS = 8
SWA_KV_HEADS = 2
SWA_HEAD_DIM = 64
SWA_BLOCK = 128
ROPE_THETA = 150000.0
MOBA_HEADS = 8
MOBA_HEAD_DIM = 64
MOBA_BLOCK = 256
MOBA_TOPK = 3
SSD_D_INNER = D_MODEL
SSD_HEAD_DIM = 64
SSD_HEADS = SSD_D_INNER // SSD_HEAD_DIM
SSD_GROUPS = 2
SSD_D_STATE = 128
SSD_CONV = 4
SSD_CHUNK = 128
X_HEADS = 4
X_HEAD_DIM = D_MODEL // X_HEADS
D_FF = 4 * D_MODEL
N_BRANCH = 3

SWA_Q_W = SWA_Q_HEADS * SWA_HEAD_DIM
SWA_KV_W = SWA_KV_HEADS * SWA_HEAD_DIM
MOBA_W = MOBA_HEADS * MOBA_HEAD_DIM
SSD_BC_W = SSD_GROUPS * SSD_D_STATE
SSD_XBC_W = SSD_D_INNER + 2 * SSD_BC_W
SSD_GROUP_W = SSD_D_INNER // SSD_GROUPS

LANES = 128
HEAD_PAIR_W = 2 * MOBA_HEAD_DIM
DT_PAD = LANES
ROW_TILE = 256
VMEM_LIMIT = 56 * 1024 * 1024

_C_QA = 0
_C_KA = _C_QA + SWA_Q_W
_C_VA = _C_KA + 2 * SWA_KV_W
_C_QB = _C_VA + 2 * SWA_KV_W
_C_KB = _C_QB + MOBA_W
_C_Z = _C_KB + MOBA_W
_C_XBC = _C_Z + SSD_D_INNER
_C_DT = _C_XBC + SSD_XBC_W
_C_END = _C_DT + DT_PAD

_NT = (((1,), (1,)), ((), ()))


def _cparams(n_axes):
    return pltpu.CompilerParams(dimension_semantics=("arbitrary",) * n_axes,
                                vmem_limit_bytes=VMEM_LIMIT)


def _rms(x, w):
    ms = jnp.mean(x * x, axis=-1, keepdims=True)
    return x * lax.rsqrt(ms + NORM_EPS) * w


def _dot(a, b):
    return jnp.dot(a, b, preferred_element_type=F32)


def _dot_nt(a, b, precision=None):
    return lax.dot_general(a, b, _NT, preferred_element_type=F32, precision=precision)


def _silu(x):
    return x * jax.nn.sigmoid(x)


def _rope_table_kernel(pos_ref, inv_ref, cos_ref, sin_ref):
    ang = pos_ref[...].astype(F32) * inv_ref[...]
    lane = lax.broadcasted_iota(jnp.int32, ang.shape, 1)
    first_half = (lane % SWA_HEAD_DIM) < (SWA_HEAD_DIM // 2)
    cos_ref[...] = jnp.cos(ang)
    sin_ref[...] = jnp.where(first_half, -jnp.sin(ang), jnp.sin(ang))


def _rope_tables(positions):
    m = positions.size
    half = SWA_HEAD_DIM // 2
    inv = ROPE_THETA ** (-jnp.arange(half, dtype=F32) / half)
    inv = jnp.tile(inv, LANES // half).reshape(1, LANES)
    t = 1024 if m % 1024 == 0 else ROW_TILE
    return pl.pallas_call(
        _rope_table_kernel,
        out_shape=(jax.ShapeDtypeStruct((m, LANES), F32), jax.ShapeDtypeStruct((m, LANES), F32)),
        grid=(m // t,),
        in_specs=[pl.BlockSpec((t, 1), lambda i: (i, 0)), pl.BlockSpec((1, LANES), lambda i: (0, 0))],
        out_specs=(pl.BlockSpec((t, LANES), lambda i: (i, 0)), pl.BlockSpec((t, LANES), lambda i: (i, 0))),
        compiler_params=_cparams(1),
        name="rope_tables",
    )(positions.reshape(m, 1), inv)


def _rope(t, cos, sin):
    w = t.shape[-1]
    reps = w // LANES
    cos_w = jnp.concatenate([cos] * reps, axis=1)
    sin_w = jnp.concatenate([sin] * reps, axis=1)
    half = SWA_HEAD_DIM // 2
    lane = lax.broadcasted_iota(jnp.int32, t.shape, 1)
    first_half = (lane % SWA_HEAD_DIM) < half
    partner = jnp.where(first_half, pltpu.roll(t, w - half, 1), pltpu.roll(t, half, 1))
    return t * cos_w + partner * sin_w


def _proj_in_kernel(nkb, x_ref, nw_ref, w_ref, wvt_ref, wdtt_ref, cos_ref, sin_ref,
                    qa_ref, ka_ref, va_ref, qb_ref, kb_ref, kmean_ref, vt_ref,
                    z_ref, xbc_ref, dt_ref, dtt_ref):
    i = pl.program_id(0)
    h = _rms(x_ref[...], nw_ref[...]).astype(BF16)
    cos = cos_ref[...]
    sin = sin_ref[...]

    def seg(a, b):
        return _dot(h, w_ref[:, a:b])

    qa_ref[...] = _rope(seg(_C_QA, _C_KA), cos, sin).astype(BF16)
    ka_ref[...] = _rope(seg(_C_KA, _C_VA), cos, sin).astype(BF16)
    va_ref[...] = seg(_C_VA, _C_QB).astype(BF16)
    qb_ref[...] = seg(_C_QB, _C_KB).astype(BF16)
    kb = seg(_C_KB, _C_Z)
    kb_ref[...] = kb.astype(BF16)
    kmean_ref[0, pl.ds(i % nkb, 1), :] = jnp.mean(kb, axis=0, keepdims=True)
    vt_ref[0] = _dot_nt(wvt_ref[...], h).astype(BF16)
    z_ref[...] = seg(_C_Z, _C_XBC).astype(BF16)
    xbc_ref[...] = seg(_C_XBC, _C_DT).astype(BF16)
    dt_ref[...] = seg(_C_DT, _C_END)
    dtt_ref[...] = _dot_nt(wdtt_ref[...], h)


def _proj_in(x2d, norm_w, wp, cos_t, sin_t, batch):
    m = x2d.shape[0]
    t = ROW_TILE
    n = m // t
    nkb = n // batch
    row = lambda w: pl.BlockSpec((t, w), lambda i: (i, 0))
    full = lambda a: pl.BlockSpec(a.shape, lambda i: (0,) * a.ndim)
    out_shape = (
        jax.ShapeDtypeStruct((m, SWA_Q_W), BF16),
        jax.ShapeDtypeStruct((m, 2 * SWA_KV_W), BF16),
        jax.ShapeDtypeStruct((m, 2 * SWA_KV_W), BF16),
        jax.ShapeDtypeStruct((m, MOBA_W), BF16),
        jax.ShapeDtypeStruct((m, MOBA_W), BF16),
        jax.ShapeDtypeStruct((batch, nkb, MOBA_W), F32),
        jax.ShapeDtypeStruct((n, MOBA_W, t), BF16),
        jax.ShapeDtypeStruct((m, SSD_D_INNER), BF16),
        jax.ShapeDtypeStruct((m, SSD_XBC_W), BF16),
        jax.ShapeDtypeStruct((m, DT_PAD), F32),
        jax.ShapeDtypeStruct((SSD_HEADS, m), F32),
    )
    out_specs = (
        row(SWA_Q_W), row(2 * SWA_KV_W), row(2 * SWA_KV_W), row(MOBA_W), row(MOBA_W),
        pl.BlockSpec((1, nkb, MOBA_W), lambda i: (i // nkb, 0, 0)),
        pl.BlockSpec((1, MOBA_W, t), lambda i: (i, 0, 0)),
        row(SSD_D_INNER), row(SSD_XBC_W), row(DT_PAD),
        pl.BlockSpec((SSD_HEADS, t), lambda i: (0, i)),
    )
    return pl.pallas_call(
        functools.partial(_proj_in_kernel, nkb),
        out_shape=out_shape,
        grid=(n,),
        in_specs=[row(D_MODEL), full(norm_w), full(wp["w_all"]), full(wp["w_vt"]), full(wp["w_dtt"]),
                  row(LANES), row(LANES)],
        out_specs=out_specs,
        compiler_params=_cparams(1),
        name="proj_in",
    )(x2d, norm_w, wp["w_all"], wp["w_vt"], wp["w_dtt"], cos_t, sin_t)


def _swa_kernel(sink_ref, q_ref, kp_ref, kc_ref, vp_ref, vc_ref, o_ref):
    i = pl.program_id(1)
    blk = SWA_BLOCK
    q = q_ref[...]
    kcat = jnp.concatenate([kp_ref[...], kc_ref[...]], axis=0)
    vcat = jnp.concatenate([vp_ref[...], vc_ref[...]], axis=0)
    qi = lax.broadcasted_iota(jnp.int32, (blk, 2 * blk), 0)
    si = lax.broadcasted_iota(jnp.int32, (blk, 2 * blk), 1)
    delta = qi + blk - si
    mask = (delta >= 0) & (delta < blk) & ((si >= blk) | (i > 0))
    lane = lax.broadcasted_iota(jnp.int32, (blk, LANES), 1)
    low = lane < SWA_HEAD_DIM
    group = SWA_Q_HEADS // SWA_KV_HEADS
    scale = SWA_HEAD_DIM ** -0.5
    outs = []
    for hd in range(SWA_Q_HEADS):
        g = hd // group
        kd = kcat[:, g * LANES:(g + 1) * LANES]
        vd = vcat[:, g * LANES:(g + 1) * LANES]
        slab = q[:, (hd // 2) * LANES:(hd // 2 + 1) * LANES]
        qm = jnp.where(low if hd % 2 == 0 else ~low, slab, jnp.zeros_like(slab))
        s = _dot_nt(qm, kd) * scale
        s = jnp.where(mask, s, -jnp.inf)
        sink = sink_ref[hd]
        mx = jnp.maximum(jnp.max(s, axis=-1, keepdims=True), sink)
        p = jnp.exp(s - mx)
        denom = jnp.sum(p, axis=-1, keepdims=True) + jnp.exp(sink - mx)
        outs.append(_dot(p.astype(BF16), vd) / denom)
    for pr in range(SWA_Q_HEADS // 2):
        o_ref[:, pr * LANES:(pr + 1) * LANES] = jnp.where(low, outs[2 * pr], outs[2 * pr + 1]).astype(BF16)


def _swa(q_a, k_a, v_a, sinks, batch):
    m = q_a.shape[0]
    blk = SWA_BLOCK
    nb = m // batch // blk
    kvw = 2 * SWA_KV_W
    cur = lambda b, i: (b * nb + i, 0)
    prev = lambda b, i: (b * nb + jnp.maximum(i - 1, 0), 0)
    return pl.pallas_call(
        _swa_kernel,
        out_shape=jax.ShapeDtypeStruct((m, SWA_Q_W), BF16),
        grid=(batch, nb),
        in_specs=[pl.BlockSpec(memory_space=pltpu.SMEM),
                  pl.BlockSpec((blk, SWA_Q_W), cur),
                  pl.BlockSpec((blk, kvw), prev), pl.BlockSpec((blk, kvw), cur),
                  pl.BlockSpec((blk, kvw), prev), pl.BlockSpec((blk, kvw), cur)],
        out_specs=pl.BlockSpec((blk, SWA_Q_W), cur),
        compiler_params=_cparams(2),
        name="swa",
    )(sinks, q_a, k_a, k_a, v_a, v_a)


def _moba_kernel(q_ref, k_ref, vt_ref, kmean_ref, o_ref, qm_ref, sel_ref, ml_ref, acc_ref):
    i = pl.program_id(2)
    blk = MOBA_BLOCK
    hd = MOBA_HEAD_DIM
    nkb = kmean_ref.shape[1]
    scale = hd ** -0.5
    q = q_ref[...]
    lane = lax.broadcasted_iota(jnp.int32, q.shape, 1)
    qm_ref[0] = jnp.where(lane < hd, q, jnp.zeros_like(q))
    qm_ref[1] = jnp.where(lane >= hd, q, jnp.zeros_like(q))

    kmean = kmean_ref[0]
    blk_id = lax.broadcasted_iota(jnp.int32, (nkb, blk), 0)
    for h in range(2):
        gate = _dot_nt(kmean, qm_ref[h].astype(F32), precision=lax.Precision.HIGHEST)
        gate = jnp.where(blk_id < i, gate, -jnp.inf)
        sel = jnp.zeros((nkb, blk), F32)
        for _ in range(min(MOBA_TOPK, nkb)):
            best = jnp.max(gate, axis=0, keepdims=True)
            idx = jnp.min(jnp.where(gate == best, blk_id, nkb), axis=0, keepdims=True)
            pick = blk_id == idx
            sel = jnp.where(pick & (blk_id < i), 1.0, sel)
            gate = jnp.where(pick, -jnp.inf, gate)
        sel_ref[h] = sel

    def tile(kj, vtj, keep, first):
        for h in range(2):
            s = _dot_nt(kj, qm_ref[h]) * scale
            s = jnp.where(keep[h], s, -jnp.inf)
            rows = slice(h * hd, (h + 1) * hd)
            if first:
                m_new = jnp.max(s, axis=0, keepdims=True)
                p = jnp.exp(s - m_new)
                ml_ref[h:h + 1, :] = m_new
                ml_ref[2 + h:3 + h, :] = jnp.sum(p, axis=0, keepdims=True)
                acc_ref[rows, :] = _dot(vtj[rows, :], p.astype(BF16))
            else:
                m_old = ml_ref[h:h + 1, :]
                m_new = jnp.maximum(m_old, jnp.max(s, axis=0, keepdims=True))
                alpha = jnp.exp(m_old - m_new)
                p = jnp.exp(s - m_new)
                ml_ref[h:h + 1, :] = m_new
                ml_ref[2 + h:3 + h, :] = alpha * ml_ref[2 + h:3 + h, :] + jnp.sum(p, axis=0, keepdims=True)
                acc_ref[rows, :] = alpha * acc_ref[rows, :] + _dot(vtj[rows, :], p.astype(BF16))

    kpos = lax.broadcasted_iota(jnp.int32, (blk, blk), 0)
    qpos = lax.broadcasted_iota(jnp.int32, (blk, blk), 1)
    causal = kpos <= qpos
    own = pl.multiple_of(i * blk, blk)
    tile(k_ref[pl.ds(own, blk), :], vt_ref[i], (causal, causal), True)

    def body(j, carry):
        start = pl.multiple_of(j * blk, blk)
        keep = tuple(sel_ref[h, pl.ds(j, 1), :] > 0.0 for h in range(2))
        tile(k_ref[pl.ds(start, blk), :], vt_ref[j], keep, False)
        return carry

    lax.fori_loop(0, i, body, 0)

    inv = jnp.concatenate([jnp.broadcast_to(1.0 / ml_ref[2:3, :], (hd, blk)),
                           jnp.broadcast_to(1.0 / ml_ref[3:4, :], (hd, blk))], axis=0)
    o_ref[...] = (acc_ref[...] * inv).T.astype(BF16)


def _moba(q_b, k_b, vt_b, kmean, batch):
    m = q_b.shape[0]
    s = m // batch
    blk = MOBA_BLOCK
    nkb = s // blk
    pairs = MOBA_HEADS // 2
    return pl.pallas_call(
        _moba_kernel,
        out_shape=jax.ShapeDtypeStruct((m, MOBA_W), BF16),
        grid=(batch, pairs, nkb),
        in_specs=[pl.BlockSpec((blk, HEAD_PAIR_W), lambda b, p, i: (b * nkb + i, p)),
                  pl.BlockSpec((s, HEAD_PAIR_W), lambda b, p, i: (b, p)),
                  pl.BlockSpec((nkb, HEAD_PAIR_W, blk), lambda b, p, i: (b, p, 0)),
                  pl.BlockSpec((1, nkb, HEAD_PAIR_W), lambda b, p, i: (b, 0, p))],
        out_specs=pl.BlockSpec((blk, HEAD_PAIR_W), lambda b, p, i: (b * nkb + i, p)),
        scratch_shapes=[pltpu.VMEM((2, blk, HEAD_PAIR_W), BF16),
                        pltpu.VMEM((2, nkb, blk), F32),
                        pltpu.VMEM((8, blk), F32),
                        pltpu.VMEM((HEAD_PAIR_W, blk), F32)],
        compiler_params=_cparams(3),
        name="moba",
    )(q_b, k_b, vt_b, kmean)


def _split_dot(v, e):
    hi = v.astype(BF16)
    lo = (v - hi.astype(F32)).astype(BF16)
    return _dot(hi, e) + _dot(lo, e)


def _ssd_kernel(z_ref, xbc_ref, dt_ref, dtt_ref, cw_ref, cb_ref, dtb_ref, dtbc_ref, alog_ref, alogc_ref,
                dskip_ref, nw_ref, exp_ref, o_ref, ext_ref, state_ref):
    c = pl.program_id(1)
    L = SSD_CHUNK
    tail = 8

    @pl.when(c == 0)
    def _():
        ext_ref[0:tail, :] = jnp.zeros((tail, SSD_XBC_W), F32)
        state_ref[...] = jnp.zeros_like(state_ref)

    ext_ref[tail:tail + L, :] = xbc_ref[...].astype(F32)
    acc = jnp.broadcast_to(cb_ref[...], (L, SSD_XBC_W))
    for j in range(SSD_CONV):
        off = tail - (SSD_CONV - 1) + j
        acc = acc + cw_ref[j:j + 1, :] * ext_ref[off:off + L, :]
    ext_ref[0:tail, :] = ext_ref[L:L + tail, :]
    xc = _silu(acc)
    xs = xc[:, :SSD_D_INNER]
    bm = xc[:, SSD_D_INNER:SSD_D_INNER + SSD_BC_W]
    cm = xc[:, SSD_D_INNER + SSD_BC_W:]

    dt = jax.nn.softplus(dt_ref[...] + dtb_ref[...])
    dtt = jax.nn.softplus(dtt_ref[...] + dtbc_ref[...])
    a_row = -jnp.exp(alog_ref[...])
    a_col = -jnp.exp(alogc_ref[...])
    r_i = lax.broadcasted_iota(jnp.int32, (L, L), 0)
    c_i = lax.broadcasted_iota(jnp.int32, (L, L), 1)
    tril = r_i >= c_i
    hp = lax.Precision.HIGHEST
    a_cs = jnp.dot(tril.astype(F32), dt * a_row, precision=hp, preferred_element_type=F32)
    a_cst = jnp.dot(dtt * a_col, (r_i <= c_i).astype(F32), precision=hp, preferred_element_type=F32)
    a_last = a_cs[L - 1:L, :]
    expand = exp_ref[...]
    decay_out = _split_dot(jnp.exp(a_cs), expand)
    decay_in = _split_dot(dt * jnp.exp(a_last - a_cs), expand)
    chunk_decay = decay_out[L - 1:L, :]

    lane = lax.broadcasted_iota(jnp.int32, (L, LANES), 1)
    low = lane < SSD_HEAD_DIM
    heads_per_group = SSD_HEADS // SSD_GROUPS
    ys = []
    for g in range(SSD_GROUPS):
        b_g = bm[:, g * SSD_D_STATE:(g + 1) * SSD_D_STATE]
        c_g = cm[:, g * SSD_D_STATE:(g + 1) * SSD_D_STATE]
        cb = _dot_nt(c_g.astype(BF16), b_g.astype(BF16))
        gcols = slice(g * SSD_GROUP_W, (g + 1) * SSD_GROUP_W)
        y_off = _dot(c_g.astype(BF16), state_ref[g].astype(BF16)) * decay_out[:, gcols]
        y_diag = []
        for pr in range(heads_per_group // 2):
            col0 = g * SSD_GROUP_W + pr * LANES
            x_pair = xs[:, col0:col0 + LANES].astype(BF16)
            halves = []
            for e in range(2):
                hidx = g * heads_per_group + 2 * pr + e
                diff = a_cs[:, hidx:hidx + 1] - a_cst[hidx:hidx + 1, :]
                wmat = cb * jnp.exp(jnp.where(tril, diff, -jnp.inf)) * dtt[hidx:hidx + 1, :]
                halves.append(_dot(wmat.astype(BF16), x_pair))
            y_diag.append(jnp.where(low, halves[0], halves[1]))
        ys.append(jnp.concatenate(y_diag, axis=1) + y_off)
        xw = (xs[:, gcols] * decay_in[:, gcols]).astype(BF16)
        state_ref[g] = state_ref[g] * chunk_decay[:, gcols] + _dot(b_g.T.astype(BF16), xw)

    y = jnp.concatenate(ys, axis=1) + xs * dskip_ref[...]
    y = y * _silu(z_ref[...].astype(F32))
    outs = []
    for g in range(SSD_GROUPS):
        yg = y[:, g * SSD_GROUP_W:(g + 1) * SSD_GROUP_W]
        outs.append(yg * lax.rsqrt(jnp.mean(yg * yg, axis=-1, keepdims=True) + NORM_EPS))
    o_ref[...] = (jnp.concatenate(outs, axis=1) * nw_ref[...]).astype(BF16)


def _ssd(z, xbc, dt, dtt, sp, batch):
    m = z.shape[0]
    L = SSD_CHUNK
    nc = m // batch // L
    row = lambda w: pl.BlockSpec((L, w), lambda b, c: (b * nc + c, 0))
    full = lambda a: pl.BlockSpec(a.shape, lambda b, c: (0,) * a.ndim)
    consts = (sp["conv_w"], sp["conv_b"], sp["dt_bias_row"], sp["dt_bias_col"], sp["a_log_row"],
              sp["a_log_col"], sp["d_skip"], sp["norm_w"], sp["expand"])
    return pl.pallas_call(
        _ssd_kernel,
        out_shape=jax.ShapeDtypeStruct((m, SSD_D_INNER), BF16),
        grid=(batch, nc),
        in_specs=[row(SSD_D_INNER), row(SSD_XBC_W), row(DT_PAD),
                  pl.BlockSpec((SSD_HEADS, L), lambda b, c: (0, b * nc + c))] + [full(a) for a in consts],
        out_specs=row(SSD_D_INNER),
        scratch_shapes=[pltpu.VMEM((L + 8, SSD_XBC_W), F32),
                        pltpu.VMEM((SSD_GROUPS, SSD_D_STATE, SSD_GROUP_W), F32)],
        compiler_params=_cparams(2),
        name="ssd",
    )(z, xbc, dt, dtt, *consts)


def _mem_kv_kernel(mem_ref, nw_ref, w_ref, k_ref, v_ref):
    hm = _rms(mem_ref[...], nw_ref[...]).astype(BF16)
    k_ref[...] = _dot(hm, w_ref[:, :D_MODEL]).astype(BF16)
    v_ref[...] = _dot(hm, w_ref[:, D_MODEL:]).astype(BF16)


def _mem_kv(mem2d, norm_w, w_ckv):
    m = mem2d.shape[0]
    t = MEM_LEN
    row = pl.BlockSpec((t, D_MODEL), lambda i: (i, 0))
    full = lambda a: pl.BlockSpec(a.shape, lambda i: (0,) * a.ndim)
    return pl.pallas_call(
        _mem_kv_kernel,
        out_shape=(jax.ShapeDtypeStruct((m, D_MODEL), BF16), jax.ShapeDtypeStruct((m, D_MODEL), BF16)),
        grid=(m // t,),
        in_specs=[row, full(norm_w), full(w_ckv)],
        out_specs=(row, row),
        compiler_params=_cparams(1),
        name="mem_kv",
    )(mem2d, norm_w, w_ckv)


def _merge_kernel(x_ref, ya_ref, yb_ref, yc_ref, nmix_ref, wg_ref, bg_ref, wa_ref, wb_ref, wc_ref, wmix_ref,
                  ncross_ref, wcq_ref, kmem_ref, vmem_ref, wco_ref, o_ref):
    x = x_ref[...]
    h = _rms(x, nmix_ref[...]).astype(BF16)
    merged = None
    for br, (y_ref, w_ref) in enumerate(((ya_ref, wa_ref), (yb_ref, wb_ref), (yc_ref, wc_ref))):
        cols = slice(br * D_MODEL, (br + 1) * D_MODEL)
        gate = jax.nn.sigmoid(_dot(h, wg_ref[:, cols]) + bg_ref[:, cols])
        term = gate * _dot(y_ref[...], w_ref[...])
        merged = term if merged is None else merged + term
    x = x + _dot(merged.astype(BF16), wmix_ref[...])

    hq = _rms(x, ncross_ref[...]).astype(BF16)
    q = _dot(hq, wcq_ref[...]).astype(BF16)
    scale = X_HEAD_DIM ** -0.5
    outs = []
    for hh in range(X_HEADS):
        cols = slice(hh * X_HEAD_DIM, (hh + 1) * X_HEAD_DIM)
        s = _dot_nt(q[:, cols], kmem_ref[:, cols]) * scale
        p = jnp.exp(s - jnp.max(s, axis=-1, keepdims=True))
        denom = jnp.sum(p, axis=-1, keepdims=True)
        outs.append(_dot(p.astype(BF16), vmem_ref[:, cols]) / denom)
    o = jnp.concatenate(outs, axis=1).astype(BF16)
    o_ref[...] = x + _dot(o, wco_ref[...])


def _merge(x2d, y_a, y_b, y_c, kmem, vmem, lp, batch):
    m = x2d.shape[0]
    t = ROW_TILE
    n = m // t
    per_b = n // batch
    row = lambda w: pl.BlockSpec((t, w), lambda i: (i, 0))
    full = lambda a: pl.BlockSpec(a.shape, lambda i: (0,) * a.ndim)
    memspec = pl.BlockSpec((MEM_LEN, D_MODEL), lambda i: (i // per_b, 0))
    return pl.pallas_call(
        _merge_kernel,
        out_shape=jax.ShapeDtypeStruct((m, D_MODEL), F32),
        grid=(n,),
        in_specs=[row(D_MODEL), row(SWA_Q_W), row(MOBA_W), row(SSD_D_INNER),
                  full(lp["norm_mix"]), full(lp["w_gate"]), full(lp["b_gate"]),
                  full(lp["w_br_swa"]), full(lp["w_br_moba"]), full(lp["w_br_ssd"]), full(lp["w_mix_out"]),
                  full(lp["norm_cross"]), full(lp["w_cq"]), memspec, memspec, full(lp["w_co"])],
        out_specs=row(D_MODEL),
        compiler_params=_cparams(1),
        name="merge_cross",
    )(x2d, y_a, y_b, y_c, lp["norm_mix"], lp["w_gate"], lp["b_gate"], lp["w_br_swa"], lp["w_br_moba"],
      lp["w_br_ssd"], lp["w_mix_out"], lp["norm_cross"], lp["w_cq"], kmem, vmem, lp["w_co"])


def _mlp_kernel(final, x_ref, nw_ref, wup_ref, wdown_ref, fnw_ref, o_ref):
    x = x_ref[...]
    hf = _rms(x, nw_ref[...]).astype(BF16)
    chunk = D_MODEL
    acc = x
    for c in range(D_FF // chunk):
        cols = slice(c * chunk, (c + 1) * chunk)
        u = jnp.maximum(_dot(hf, wup_ref[:, cols]), 0.0)
        acc = acc + _dot((u * u).astype(BF16), wdown_ref[cols, :])
    if final:
        acc = _rms(acc, fnw_ref[...])
    o_ref[...] = acc


def _mlp(x2d, norm_w, w_up, w_down, final_norm_w, final):
    m = x2d.shape[0]
    t = ROW_TILE
    row = pl.BlockSpec((t, D_MODEL), lambda i: (i, 0))
    full = lambda a: pl.BlockSpec(a.shape, lambda i: (0,) * a.ndim)
    return pl.pallas_call(
        functools.partial(_mlp_kernel, final),
        out_shape=jax.ShapeDtypeStruct((m, D_MODEL), F32),
        grid=(m // t,),
        in_specs=[row, full(norm_w), full(w_up), full(w_down), full(final_norm_w)],
        out_specs=row,
        compiler_params=_cparams(1),
        name="mlp_final" if final else "mlp",
    )(x2d, norm_w, w_up, w_down, final_norm_w)


def _dup_heads(w, n_heads, hd):
    w = w.reshape(w.shape[0], n_heads, 1, hd)
    return jnp.broadcast_to(w, (w.shape[0], n_heads, 2, hd)).reshape(w.shape[0], n_heads * 2 * hd)


def _prep_layer(l, p):
    w_in = p["w_in"][l]
    o = 0
    parts = {}
    for name, width in (("qa", SWA_Q_W), ("ka", SWA_KV_W), ("va", SWA_KV_W), ("qb", MOBA_W), ("kb", MOBA_W),
                        ("vb", MOBA_W), ("z", SSD_D_INNER), ("xbc", SSD_XBC_W), ("dt", SSD_HEADS)):
        parts[name] = w_in[:, o:o + width]
        o += width
    dt_pad = jnp.pad(parts["dt"], ((0, 0), (0, DT_PAD - SSD_HEADS)))
    w_all = jnp.concatenate([parts["qa"], _dup_heads(parts["ka"], SWA_KV_HEADS, SWA_HEAD_DIM),
                             _dup_heads(parts["va"], SWA_KV_HEADS, SWA_HEAD_DIM), parts["qb"], parts["kb"],
                             parts["z"], parts["xbc"], dt_pad], axis=1).astype(BF16)
    row = lambda v: v.reshape(1, -1).astype(F32)
    pad_row = lambda v: jnp.pad(v.astype(F32), (0, DT_PAD - SSD_HEADS)).reshape(1, DT_PAD)
    head_of_col = jnp.arange(SSD_D_INNER) // SSD_HEAD_DIM
    expand = (jnp.arange(DT_PAD)[:, None] == head_of_col[None, :]).astype(BF16)
    ssd = {
        "conv_w": p["conv_w"][l].astype(F32), "conv_b": row(p["conv_b"][l]),
        "dt_bias_row": pad_row(p["dt_bias"][l]), "dt_bias_col": p["dt_bias"][l].reshape(SSD_HEADS, 1).astype(F32),
        "a_log_row": pad_row(p["a_log"][l]), "a_log_col": p["a_log"][l].reshape(SSD_HEADS, 1).astype(F32),
        "d_skip": row(jnp.repeat(p["d_skip"][l], SSD_HEAD_DIM)), "norm_w": row(p["ssd_norm"][l]),
        "expand": expand,
    }
    return {
        "proj": {"w_all": w_all, "w_vt": parts["vb"].T.astype(BF16), "w_dtt": parts["dt"].T.astype(BF16)},
        "ssd": ssd,
        "norm_mix": row(p["norm_mix"][l]), "w_gate": p["w_gate"][l].astype(BF16), "b_gate": row(p["b_gate"][l]),
        "swa_sinks": p["swa_sinks"][l].astype(F32),
        "w_br_swa": p["w_br_swa"][l].astype(BF16), "w_br_moba": p["w_br_moba"][l].astype(BF16),
        "w_br_ssd": p["w_br_ssd"][l].astype(BF16), "w_mix_out": p["w_mix_out"][l].astype(BF16),
        "norm_cross": row(p["norm_cross"][l]), "norm_mem": row(p["norm_mem"][l]),
        "w_cq": p["w_cq"][l].astype(BF16), "w_ckv": p["w_ckv"][l].astype(BF16), "w_co": p["w_co"][l].astype(BF16),
        "norm_mlp": row(p["norm_mlp"][l]), "w_up": p["w_up"][l].astype(BF16), "w_down": p["w_down"][l].astype(BF16),
    }


def kernel(x, mem, positions, norm_mix, w_in, w_gate, b_gate, swa_sinks, conv_w, conv_b, dt_bias, a_log, d_skip,
           ssd_norm, w_br_swa, w_br_moba, w_br_ssd, w_mix_out, norm_cross, norm_mem, w_cq, w_ckv, w_co, norm_mlp,
           w_up, w_down, final_norm):
    batch, seq, d = x.shape
    depth = w_in.shape[0]
    assert d == D_MODEL and seq % MOBA_BLOCK == 0 and mem.shape[1] == MEM_LEN
    params = dict(norm_mix=norm_mix, w_in=w_in, w_gate=w_gate, b_gate=b_gate, swa_sinks=swa_sinks, conv_w=conv_w,
                  conv_b=conv_b, dt_bias=dt_bias, a_log=a_log, d_skip=d_skip, ssd_norm=ssd_norm, w_br_swa=w_br_swa,
                  w_br_moba=w_br_moba, w_br_ssd=w_br_ssd, w_mix_out=w_mix_out, norm_cross=norm_cross,
                  norm_mem=norm_mem, w_cq=w_cq, w_ckv=w_ckv, w_co=w_co, norm_mlp=norm_mlp, w_up=w_up, w_down=w_down)
    x2d = x.reshape(batch * seq, d)
    mem2d = mem.reshape(batch * MEM_LEN, d)
    fnw = final_norm.reshape(1, d).astype(F32)
    cos_t, sin_t = _rope_tables(positions)
    for l in range(depth):
        lp = _prep_layer(l, params)
        (q_a, k_a, v_a, q_b, k_b, kmean, vt_b, z, xbc, dt, dtt) = _proj_in(
            x2d, lp["norm_mix"], lp["proj"], cos_t, sin_t, batch)
        y_a = _swa(q_a, k_a, v_a, lp["swa_sinks"], batch)
        y_b = _moba(q_b, k_b, vt_b, kmean, batch)
        y_c = _ssd(z, xbc, dt, dtt, lp["ssd"], batch)
        kmem, vmem = _mem_kv(mem2d, lp["norm_mem"], lp["w_ckv"])
        x2d = _merge(x2d, y_a, y_b, y_c, kmem, vmem, lp, batch)
        x2d = _mlp(x2d, lp["norm_mlp"], lp["w_up"], lp["w_down"], fnw, final=(l == depth - 1))
    return x2d.reshape(batch, seq, d)
```

```python
import functools
import math

import jax
import jax.numpy as jnp
from jax import lax
from jax.experimental import pallas as pl
from jax.experimental.pallas import tpu as pltpu

F32 = jnp.float32
BF16 = jnp.bfloat16

D_MODEL = 1024
MEM_LEN = 256
NORM_EPS = 1e-6
SWA_Q_HEADS = 8
SWA_KV_HEADS = 2
SWA_HEAD_DIM = 64
SWA_BLOCK = 128
ROPE_THETA = 150000.0
MOBA_HEADS = 8
MOBA_HEAD_DIM = 64
MOBA_BLOCK = 256
MOBA_TOPK = 3
MOBA_GROUP = 4
SUM_ROWS = 16
SSD_D_INNER = D_MODEL
SSD_HEAD_DIM = 64
SSD_HEADS = SSD_D_INNER // SSD_HEAD_DIM
SSD_GROUPS = 2
SSD_D_STATE = 128
SSD_CONV = 4
SSD_CHUNK = 128
X_HEADS = 4
X_HEAD_DIM = D_MODEL // X_HEADS
D_FF = 4 * D_MODEL
N_BRANCH = 3

SWA_Q_W = SWA_Q_HEADS * SWA_HEAD_DIM
SWA_KV_W = SWA_KV_HEADS * SWA_HEAD_DIM
MOBA_W = MOBA_HEADS * MOBA_HEAD_DIM
SSD_BC_W = SSD_GROUPS * SSD_D_STATE
SSD_XBC_W = SSD_D_INNER + 2 * SSD_BC_W
SSD_GROUP_W = SSD_D_INNER // SSD_GROUPS

LANES = 128
HEAD_PAIR_W = 2 * MOBA_HEAD_DIM
DT_PAD = LANES
ROW_TILE = 256
VMEM_LIMIT = 56 * 1024 * 1024

_C_QA = 0
_C_KA = _C_QA + SWA_Q_W
_C_VA = _C_KA + 2 * SWA_KV_W
_C_QB = _C_VA + 2 * SWA_KV_W
_C_KB = _C_QB + MOBA_W
_C_Z = _C_KB + MOBA_W
_C_XBC = _C_Z + SSD_D_INNER
_C_DT = _C_XBC + SSD_XBC_W
_C_END = _C_DT + DT_PAD

LOG2E = math.log2(math.e)
_NT = (((1,), (1,)), ((), ()))


def _cparams(n_axes, flags=None):
    return pltpu.CompilerParams(dimension_semantics=("arbitrary",) * n_axes,
                                vmem_limit_bytes=VMEM_LIMIT, flags=flags)


def _rms(x, w):
    ms = jnp.mean(x * x, axis=-1, keepdims=True)
    return x * lax.rsqrt(ms + NORM_EPS) * w


def _dot(a, b):
    return jnp.dot(a, b, preferred_element_type=F32)


def _dot_nt(a, b, precision=None):
    return lax.dot_general(a, b, _NT, preferred_element_type=F32, precision=precision)


def _silu(x):
    return x * jax.nn.sigmoid(x)


def _rope_table_kernel(pos_ref, inv_ref, cos_ref, sin_ref):
    ang = pos_ref[...].astype(F32) * inv_ref[...]
    lane = lax.broadcasted_iota(jnp.int32, ang.shape, 1)
    first_half = (lane % SWA_HEAD_DIM) < (SWA_HEAD_DIM // 2)
    cos_ref[...] = jnp.cos(ang)
    sin_ref[...] = jnp.where(first_half, -jnp.sin(ang), jnp.sin(ang))


def _rope_tables(positions):
    m = positions.size
    half = SWA_HEAD_DIM // 2
    inv = ROPE_THETA ** (-jnp.arange(half, dtype=F32) / half)
    inv = jnp.tile(inv, LANES // half).reshape(1, LANES)
    t = 1024 if m % 1024 == 0 else ROW_TILE
    return pl.pallas_call(
        _rope_table_kernel,
        out_shape=(jax.ShapeDtypeStruct((m, LANES), F32), jax.ShapeDtypeStruct((m, LANES), F32)),
        grid=(m // t,),
        in_specs=[pl.BlockSpec((t, 1), lambda i: (i, 0)), pl.BlockSpec((1, LANES), lambda i: (0, 0))],
        out_specs=(pl.BlockSpec((t, LANES), lambda i: (i, 0)), pl.BlockSpec((t, LANES), lambda i: (i, 0))),
        compiler_params=_cparams(1),
        name="rope_tables",
    )(positions.reshape(m, 1), inv)


def _rope(t, cos, sin):
    w = t.shape[-1]
    reps = w // LANES
    cos_w = jnp.concatenate([cos] * reps, axis=1)
    sin_w = jnp.concatenate([sin] * reps, axis=1)
    half = SWA_HEAD_DIM // 2
    lane = lax.broadcasted_iota(jnp.int32, t.shape, 1)
    first_half = (lane % SWA_HEAD_DIM) < half
    partner = jnp.where(first_half, pltpu.roll(t, w - half, 1), pltpu.roll(t, half, 1))
    return t * cos_w + partner * sin_w


def _proj_in_kernel(nkb, x_ref, nw_ref, w_ref, wvt_ref, wdtt_ref, cos_ref, sin_ref,
                    qa_ref, ka_ref, va_ref, qb_ref, kb_ref, kmean_ref, vt_ref,
                    z_ref, xbc_ref, dt_ref, dtt_ref):
    i = pl.program_id(0)
    h = _rms(x_ref[...], nw_ref[...]).astype(BF16)
    cos = cos_ref[...]
    sin = sin_ref[...]

    def seg(a, b):
        return _dot(h, w_ref[:, a:b])

    qa_ref[...] = _rope(seg(_C_QA, _C_KA), cos, sin).astype(BF16)
    ka_ref[...] = _rope(seg(_C_KA, _C_VA), cos, sin).astype(BF16)
    va_ref[...] = seg(_C_VA, _C_QB).astype(BF16)
    qb_ref[...] = seg(_C_QB, _C_KB).astype(BF16)
    kb = seg(_C_KB, _C_Z)
    kb_ref[...] = (kb * LOG2E).astype(BF16)
    kmean_ref[0, pl.ds(i % nkb, 1), :] = jnp.mean(kb, axis=0, keepdims=True)
    vt_ref[...] = _dot_nt(wvt_ref[...], h).astype(BF16)
    z_ref[...] = seg(_C_Z, _C_XBC).astype(BF16)
    xbc_ref[...] = seg(_C_XBC, _C_DT).astype(BF16)
    dt_ref[...] = seg(_C_DT, _C_END)
    dtt_ref[...] = _dot_nt(wdtt_ref[...], h)


def _proj_in(x2d, norm_w, wp, cos_t, sin_t, batch):
    m = x2d.shape[0]
    t = ROW_TILE
    n = m // t
    nkb = n // batch
    row = lambda w: pl.BlockSpec((t, w), lambda i: (i, 0))
    full = lambda a: pl.BlockSpec(a.shape, lambda i: (0,) * a.ndim)
    out_shape = (
        jax.ShapeDtypeStruct((m, SWA_Q_W), BF16),
        jax.ShapeDtypeStruct((m, 2 * SWA_KV_W), BF16),
        jax.ShapeDtypeStruct((m, 2 * SWA_KV_W), BF16),
        jax.ShapeDtypeStruct((m, MOBA_W), BF16),
        jax.ShapeDtypeStruct((m, MOBA_W), BF16),
        jax.ShapeDtypeStruct((batch, nkb, MOBA_W), F32),
        jax.ShapeDtypeStruct((MOBA_W, m), BF16),
        jax.ShapeDtypeStruct((m, SSD_D_INNER), BF16),
        jax.ShapeDtypeStruct((m, SSD_XBC_W), BF16),
        jax.ShapeDtypeStruct((m, DT_PAD), F32),
        jax.ShapeDtypeStruct((SSD_HEADS, m), F32),
    )
    out_specs = (
        row(SWA_Q_W), row(2 * SWA_KV_W), row(2 * SWA_KV_W), row(MOBA_W), row(MOBA_W),
        pl.BlockSpec((1, nkb, MOBA_W), lambda i: (i // nkb, 0, 0)),
        pl.BlockSpec((MOBA_W, t), lambda i: (0, i)),
        row(SSD_D_INNER), row(SSD_XBC_W), row(DT_PAD),
        pl.BlockSpec((SSD_HEADS, t), lambda i: (0, i)),
    )
    return pl.pallas_call(
        functools.partial(_proj_in_kernel, nkb),
        out_shape=out_shape,
        grid=(n,),
        in_specs=[row(D_MODEL), full(norm_w), full(wp["w_all"]), full(wp["w_vt"]), full(wp["w_dtt"]),
                  row(LANES), row(LANES)],
        out_specs=out_specs,
        compiler_params=_cparams(1),
        name="proj_in",
    )(x2d, norm_w, wp["w_all"], wp["w_vt"], wp["w_dtt"], cos_t, sin_t)


def _swa_kernel(sink_ref, q_ref, kp_ref, kc_ref, vp_ref, vc_ref, o_ref):
    i = pl.program_id(1)
    blk = SWA_BLOCK
    q = q_ref[...]
    kcat = jnp.concatenate([kp_ref[...], kc_ref[...]], axis=0)
    vcat = jnp.concatenate([vp_ref[...], vc_ref[...]], axis=0)
    qi = lax.broadcasted_iota(jnp.int32, (blk, 2 * blk), 0)
    si = lax.broadcasted_iota(jnp.int32, (blk, 2 * blk), 1)
    delta = qi + blk - si
    mask = (delta >= 0) & (delta < blk) & ((si >= blk) | (i > 0))
    lane = lax.broadcasted_iota(jnp.int32, (blk, LANES), 1)
    low = lane < SWA_HEAD_DIM
    group = SWA_Q_HEADS // SWA_KV_HEADS
    scale = SWA_HEAD_DIM ** -0.5
    outs = []
    for hd in range(SWA_Q_HEADS):
        g = hd // group
        kd = kcat[:, g * LANES:(g + 1) * LANES]
        vd = vcat[:, g * LANES:(g + 1) * LANES]
        slab = q[:, (hd // 2) * LANES:(hd // 2 + 1) * LANES]
        qm = jnp.where(low if hd % 2 == 0 else ~low, slab, jnp.zeros_like(slab))
        s = _dot_nt(qm, kd) * scale
        s = jnp.where(mask, s, -jnp.inf)
        sink = sink_ref[hd]
        mx = jnp.maximum(jnp.max(s, axis=-1, keepdims=True), sink)
        p = jnp.exp(s - mx)
        denom = jnp.sum(p, axis=-1, keepdims=True) + jnp.exp(sink - mx)
        outs.append(_dot(p.astype(BF16), vd) / denom)
    for pr in range(SWA_Q_HEADS // 2):
        o_ref[:, pr * LANES:(pr + 1) * LANES] = jnp.where(low, outs[2 * pr], outs[2 * pr + 1]).astype(BF16)


def _swa(q_a, k_a, v_a, sinks, batch):
    m = q_a.shape[0]
    blk = SWA_BLOCK
    nb = m // batch // blk
    kvw = 2 * SWA_KV_W
    cur = lambda b, i: (b * nb + i, 0)
    prev = lambda b, i: (b * nb + jnp.maximum(i - 1, 0), 0)
    return pl.pallas_call(
        _swa_kernel,
        out_shape=jax.ShapeDtypeStruct((m, SWA_Q_W), BF16),
        grid=(batch, nb),
        in_specs=[pl.BlockSpec(memory_space=pltpu.SMEM),
                  pl.BlockSpec((blk, SWA_Q_W), cur),
                  pl.BlockSpec((blk, kvw), prev), pl.BlockSpec((blk, kvw), cur),
                  pl.BlockSpec((blk, kvw), prev), pl.BlockSpec((blk, kvw), cur)],
        out_specs=pl.BlockSpec((blk, SWA_Q_W), cur),
        compiler_params=_cparams(2),
        name="swa",
    )(sinks, q_a, k_a, k_a, v_a, v_a)


def _fold8(x, op):
    return functools.reduce(op, [x[r:r + 8] for r in range(0, x.shape[0], 8)])


def _moba_kernel(q_ref, k_ref, vt_ref, kmean_ref, o_ref, qm_ref, sel_ref, ml_ref, acc_ref,
                 so_ref, mxo_ref, po_ref, sa_ref, mxa_ref, pa_ref, sb_ref, mxb_ref, pb_ref):
    i = pl.program_id(2)
    blk = MOBA_BLOCK
    hd = MOBA_HEAD_DIM
    grp = MOBA_GROUP
    nkb = kmean_ref.shape[1]
    q = q_ref[...] * (hd ** -0.5)
    lane = lax.broadcasted_iota(jnp.int32, q.shape, 1)
    qm_ref[0:blk, :] = jnp.where(lane < hd, q, jnp.zeros_like(q))
    qm_ref[blk:2 * blk, :] = jnp.where(lane >= hd, q, jnp.zeros_like(q))

    blk_id = lax.broadcasted_iota(jnp.int32, (nkb, 2 * blk), 0)
    gate = _dot_nt(kmean_ref[0], qm_ref[...].astype(F32), precision=lax.Precision.HIGHEST)
    gate = jnp.where(blk_id < i, gate, -jnp.inf)
    sel = jnp.zeros((nkb, 2 * blk), F32)
    for _ in range(min(MOBA_TOPK, nkb)):
        best = jnp.max(gate, axis=0, keepdims=True)
        idx = jnp.min(jnp.where(gate == best, blk_id, nkb), axis=0, keepdims=True)
        pick = blk_id == idx
        sel = jnp.where(pick & (blk_id < i), 1.0, sel)
        gate = jnp.where(pick, -jnp.inf, gate)
    sel_ref[0:nkb, :] = sel
    sel_ref[nkb:nkb + grp, :] = jnp.zeros((grp, 2 * blk), F32)

    def scores(start, keeps, s_buf, mx_buf):
        mx = None
        for r, keep in enumerate(keeps):
            s = _dot_nt(k_ref[pl.ds(start + r * blk, blk), :], qm_ref[...])
            s = jnp.where(keep, s, -jnp.inf)
            s_buf[r * blk:(r + 1) * blk, :] = s
            m8 = _fold8(s, jnp.maximum)
            mx = m8 if mx is None else jnp.maximum(mx, m8)
        mx_buf[...] = mx

    def accumulate(start, n_blocks, s_buf, mx_buf, p_buf, first):
        n = n_blocks * blk
        mx = jnp.max(mx_buf[...], axis=0, keepdims=True)
        if first:
            m_new = mx
        else:
            m_old = ml_ref[0:1, :]
            m_new = jnp.maximum(m_old, mx)
            alpha = jnp.exp2(m_old - m_new)
        for r in range(n_blocks):
            p_buf[r * blk:(r + 1) * blk, :] = jnp.exp2(s_buf[r * blk:(r + 1) * blk, :] - m_new).astype(BF16)
        ml_ref[0:1, :] = m_new
        ones = jnp.ones((SUM_ROWS, n), BF16)
        for h in range(2):
            cols = slice(h * blk, (h + 1) * blk)
            vt_ext = jnp.concatenate([vt_ref[h * hd:(h + 1) * hd, pl.ds(start, n)], ones], axis=0)
            pv = _dot(vt_ext, p_buf[0:n, cols])
            acc_ref[h] = pv if first else alpha[:, cols] * acc_ref[h] + pv

    n_groups = (i + grp - 1) // grp

    def group_start(g):
        return pl.multiple_of(jnp.minimum(g, nkb // grp - 1) * (grp * blk), grp * blk)

    def group_scores(g, s_buf, mx_buf):
        keeps = tuple(sel_ref[pl.ds(g * grp + r, 1), :] > 0.0 for r in range(grp))
        scores(group_start(g), keeps, s_buf, mx_buf)

    kpos = lax.broadcasted_iota(jnp.int32, (blk, 2 * blk), 0)
    qpos = lax.broadcasted_iota(jnp.int32, (blk, 2 * blk), 1) % blk
    own = pl.multiple_of(i * blk, blk)
    scores(own, (kpos <= qpos,), so_ref, mxo_ref)
    group_scores(0, sa_ref, mxa_ref)
    accumulate(own, 1, so_ref, mxo_ref, po_ref, True)

    def pair(u, carry):
        g = 2 * u
        group_scores(g + 1, sb_ref, mxb_ref)
        accumulate(group_start(g), grp, sa_ref, mxa_ref, pa_ref, False)
        group_scores(g + 2, sa_ref, mxa_ref)
        accumulate(group_start(g + 1), grp, sb_ref, mxb_ref, pb_ref, False)
        return carry

    lax.fori_loop(0, n_groups // 2, pair, 0)
    accumulate(group_start(2 * (n_groups // 2)), grp, sa_ref, mxa_ref, pa_ref, False)

    out_t = jnp.concatenate([acc_ref[h, 0:hd, :] / acc_ref[h, hd:hd + 1, :] for h in range(2)], axis=0)
    o_ref[...] = out_t.T.astype(BF16)


def _moba_stage(n_blocks, blk):
    return [pltpu.VMEM((n_blocks * blk, 2 * blk), F32), pltpu.VMEM((8, 2 * blk), F32),
            pltpu.VMEM((n_blocks * blk, 2 * blk), BF16)]


def _moba(q_b, k_b, vt_b, kmean, batch):
    m = q_b.shape[0]
    s = m // batch
    blk = MOBA_BLOCK
    nkb = s // blk
    assert nkb % MOBA_GROUP == 0
    pairs = MOBA_HEADS // 2
    return pl.pallas_call(
        _moba_kernel,
        out_shape=jax.ShapeDtypeStruct((m, MOBA_W), BF16),
        grid=(batch, pairs, nkb),
        in_specs=[pl.BlockSpec((blk, HEAD_PAIR_W), lambda b, p, i: (b * nkb + i, p)),
                  pl.BlockSpec((s, HEAD_PAIR_W), lambda b, p, i: (b, p)),
                  pl.BlockSpec((HEAD_PAIR_W, s), lambda b, p, i: (p, b)),
                  pl.BlockSpec((1, nkb, HEAD_PAIR_W), lambda b, p, i: (b, 0, p))],
        out_specs=pl.BlockSpec((blk, HEAD_PAIR_W), lambda b, p, i: (b * nkb + i, p)),
        scratch_shapes=[pltpu.VMEM((2 * blk, HEAD_PAIR_W), BF16),
                        pltpu.VMEM((nkb + MOBA_GROUP, 2 * blk), F32),
                        pltpu.VMEM((8, 2 * blk), F32),
                        pltpu.VMEM((2, MOBA_HEAD_DIM + SUM_ROWS, blk), F32),
                        ] + _moba_stage(1, blk) + _moba_stage(MOBA_GROUP, blk) + _moba_stage(MOBA_GROUP, blk),
        compiler_params=_cparams(3),
        name="moba",
    )(q_b, k_b, vt_b, kmean)


def _split_dot(v, e):
    hi = v.astype(BF16)
    lo = (v - hi.astype(F32)).astype(BF16)
    return _dot(hi, e) + _dot(lo, e)


def _ssd_kernel(z_ref, xbc_ref, dt_ref, dtt_ref, cw_ref, cb_ref, dtb_ref, dtbc_ref, alog_ref, alogc_ref,
                dskip_ref, nw_ref, exp_ref, o_ref, ext_ref, state_ref):
    c = pl.program_id(1)
    L = SSD_CHUNK
    tail = 8

    @pl.when(c == 0)
    def _():
        ext_ref[0:tail, :] = jnp.zeros((tail, SSD_XBC_W), F32)
        state_ref[...] = jnp.zeros_like(state_ref)

    ext_ref[tail:tail + L, :] = xbc_ref[...].astype(F32)
    acc = jnp.broadcast_to(cb_ref[...], (L, SSD_XBC_W))
    for j in range(SSD_CONV):
        off = tail - (SSD_CONV - 1) + j
        acc = acc + cw_ref[j:j + 1, :] * ext_ref[off:off + L, :]
    ext_ref[0:tail, :] = ext_ref[L:L + tail, :]
    xc = _silu(acc)
    xs = xc[:, :SSD_D_INNER]
    bm = xc[:, SSD_D_INNER:SSD_D_INNER + SSD_BC_W]
    cm = xc[:, SSD_D_INNER + SSD_BC_W:]

    dt = jax.nn.softplus(dt_ref[...] + dtb_ref[...])
    dtt = jax.nn.softplus(dtt_ref[...] + dtbc_ref[...])
    a_row = -jnp.exp(alog_ref[...])
    a_col = -jnp.exp(alogc_ref[...])
    r_i = lax.broadcasted_iota(jnp.int32, (L, L), 0)
    c_i = lax.broadcasted_iota(jnp.int32, (L, L), 1)
    tril = r_i >= c_i
    hp = lax.Precision.HIGHEST
    a_cs = jnp.dot(tril.astype(F32), dt * a_row, precision=hp, preferred_element_type=F32)
    a_cst = jnp.dot(dtt * a_col, (r_i <= c_i).astype(F32), precision=hp, preferred_element_type=F32)
    a_last = a_cs[L - 1:L, :]
    expand = exp_ref[...]
    decay_out = _split_dot(jnp.exp(a_cs), expand)
    decay_in = _split_dot(dt * jnp.exp(a_last - a_cs), expand)
    chunk_decay = decay_out[L - 1:L, :]

    lane = lax.broadcasted_iota(jnp.int32, (L, LANES), 1)
    low = lane < SSD_HEAD_DIM
    heads_per_group = SSD_HEADS // SSD_GROUPS
    ys = []
    for g in range(SSD_GROUPS):
        b_g = bm[:, g * SSD_D_STATE:(g + 1) * SSD_D_STATE]
        c_g = cm[:, g * SSD_D_STATE:(g + 1) * SSD_D_STATE]
        cb = _dot_nt(c_g.astype(BF16), b_g.astype(BF16))
        gcols = slice(g * SSD_GROUP_W, (g + 1) * SSD_GROUP_W)
        y_off = _dot(c_g.astype(BF16), state_ref[g].astype(BF16)) * decay_out[:, gcols]
        y_diag = []
        for pr in range(heads_per_group // 2):
            col0 = g * SSD_GROUP_W + pr * LANES
            x_pair = xs[:, col0:col0 + LANES].astype(BF16)
            halves = []
            for e in range(2):
                hidx = g * heads_per_group + 2 * pr + e
                diff = a_cs[:, hidx:hidx + 1] - a_cst[hidx:hidx + 1, :]
                wmat = cb * jnp.exp(jnp.where(tril, diff, -jnp.inf)) * dtt[hidx:hidx + 1, :]
                halves.append(_dot(wmat.astype(BF16), x_pair))
            y_diag.append(jnp.where(low, halves[0], halves[1]))
        ys.append(jnp.concatenate(y_diag, axis=1) + y_off)
        xw = (xs[:, gcols] * decay_in[:, gcols]).astype(BF16)
        state_ref[g] = state_ref[g] * chunk_decay[:, gcols] + _dot(b_g.T.astype(BF16), xw)

    y = jnp.concatenate(ys, axis=1) + xs * dskip_ref[...]
    y = y * _silu(z_ref[...].astype(F32))
    outs = []
    for g in range(SSD_GROUPS):
        yg = y[:, g * SSD_GROUP_W:(g + 1) * SSD_GROUP_W]
        outs.append(yg * lax.rsqrt(jnp.mean(yg * yg, axis=-1, keepdims=True) + NORM_EPS))
    o_ref[...] = (jnp.concatenate(outs, axis=1) * nw_ref[...]).astype(BF16)


def _ssd(z, xbc, dt, dtt, sp, batch):
    m = z.shape[0]
    L = SSD_CHUNK
    nc = m // batch // L
    row = lambda w: pl.BlockSpec((L, w), lambda b, c: (b * nc + c, 0))
    full = lambda a: pl.BlockSpec(a.shape, lambda b, c: (0,) * a.ndim)
    consts = (sp["conv_w"], sp["conv_b"], sp["dt_bias_row"], sp["dt_bias_col"], sp["a_log_row"],
              sp["a_log_col"], sp["d_skip"], sp["norm_w"], sp["expand"])
    return pl.pallas_call(
        _ssd_kernel,
        out_shape=jax.ShapeDtypeStruct((m, SSD_D_INNER), BF16),
        grid=(batch, nc),
        in_specs=[row(SSD_D_INNER), row(SSD_XBC_W), row(DT_PAD),
                  pl.BlockSpec((SSD_HEADS, L), lambda b, c: (0, b * nc + c))] + [full(a) for a in consts],
        out_specs=row(SSD_D_INNER),
        scratch_shapes=[pltpu.VMEM((L + 8, SSD_XBC_W), F32),
                        pltpu.VMEM((SSD_GROUPS, SSD_D_STATE, SSD_GROUP_W), F32)],
        compiler_params=_cparams(2),
        name="ssd",
    )(z, xbc, dt, dtt, *consts)


def _mem_kv_kernel(mem_ref, nw_ref, w_ref, k_ref, v_ref):
    hm = _rms(mem_ref[...], nw_ref[...]).astype(BF16)
    k_ref[...] = _dot(hm, w_ref[:, :D_MODEL]).astype(BF16)
    v_ref[...] = _dot(hm, w_ref[:, D_MODEL:]).astype(BF16)


def _mem_kv(mem2d, norm_w, w_ckv):
    m = mem2d.shape[0]
    t = MEM_LEN
    row = pl.BlockSpec((t, D_MODEL), lambda i: (i, 0))
    full = lambda a: pl.BlockSpec(a.shape, lambda i: (0,) * a.ndim)
    return pl.pallas_call(
        _mem_kv_kernel,
        out_shape=(jax.ShapeDtypeStruct((m, D_MODEL), BF16), jax.ShapeDtypeStruct((m, D_MODEL), BF16)),
        grid=(m // t,),
        in_specs=[row, full(norm_w), full(w_ckv)],
        out_specs=(row, row),
        compiler_params=_cparams(1),
        name="mem_kv",
    )(mem2d, norm_w, w_ckv)


def _merge_kernel(x_ref, ya_ref, yb_ref, yc_ref, nmix_ref, wg_ref, bg_ref, wa_ref, wb_ref, wc_ref, wmix_ref,
                  ncross_ref, wcq_ref, kmem_ref, vmem_ref, wco_ref, o_ref):
    x = x_ref[...]
    h = _rms(x, nmix_ref[...]).astype(BF16)
    merged = None
    for br, (y_ref, w_ref) in enumerate(((ya_ref, wa_ref), (yb_ref, wb_ref), (yc_ref, wc_ref))):
        cols = slice(br * D_MODEL, (br + 1) * D_MODEL)
        gate = jax.nn.sigmoid(_dot(h, wg_ref[:, cols]) + bg_ref[:, cols])
        term = gate * _dot(y_ref[...], w_ref[...])
        merged = term if merged is None else merged + term
    x = x + _dot(merged.astype(BF16), wmix_ref[...])

    hq = _rms(x, ncross_ref[...]).astype(BF16)
    q = _dot(hq, wcq_ref[...]).astype(BF16)
    scale = X_HEAD_DIM ** -0.5
    outs = []
    for hh in range(X_HEADS):
        cols = slice(hh * X_HEAD_DIM, (hh + 1) * X_HEAD_DIM)
        s = _dot_nt(q[:, cols], kmem_ref[:, cols]) * scale
        p = jnp.exp(s - jnp.max(s, axis=-1, keepdims=True))
        denom = jnp.sum(p, axis=-1, keepdims=True)
        outs.append(_dot(p.astype(BF16), vmem_ref[:, cols]) / denom)
    o = jnp.concatenate(outs, axis=1).astype(BF16)
    o_ref[...] = x + _dot(o, wco_ref[...])


def _merge(x2d, y_a, y_b, y_c, kmem, vmem, lp, batch):
    m = x2d.shape[0]
    t = ROW_TILE
    n = m // t
    per_b = n // batch
    row = lambda w: pl.BlockSpec((t, w), lambda i: (i, 0))
    full = lambda a: pl.BlockSpec(a.shape, lambda i: (0,) * a.ndim)
    memspec = pl.BlockSpec((MEM_LEN, D_MODEL), lambda i: (i // per_b, 0))
    return pl.pallas_call(
        _merge_kernel,
        out_shape=jax.ShapeDtypeStruct((m, D_MODEL), F32),
        grid=(n,),
        in_specs=[row(D_MODEL), row(SWA_Q_W), row(MOBA_W), row(SSD_D_INNER),
                  full(lp["norm_mix"]), full(lp["w_gate"]), full(lp["b_gate"]),
                  full(lp["w_br_swa"]), full(lp["w_br_moba"]), full(lp["w_br_ssd"]), full(lp["w_mix_out"]),
                  full(lp["norm_cross"]), full(lp["w_cq"]), memspec, memspec, full(lp["w_co"])],
        out_specs=row(D_MODEL),
        compiler_params=_cparams(1),
        name="merge_cross",
    )(x2d, y_a, y_b, y_c, lp["norm_mix"], lp["w_gate"], lp["b_gate"], lp["w_br_swa"], lp["w_br_moba"],
      lp["w_br_ssd"], lp["w_mix_out"], lp["norm_cross"], lp["w_cq"], kmem, vmem, lp["w_co"])


def _mlp_kernel(final, x_ref, nw_ref, wup_ref, wdown_ref, fnw_ref, o_ref):
    x = x_ref[...]
    hf = _rms(x, nw_ref[...]).astype(BF16)
    chunk = D_MODEL
    acc = x
    for c in range(D_FF // chunk):
        cols = slice(c * chunk, (c + 1) * chunk)
        u = jnp.maximum(_dot(hf, wup_ref[:, cols]), 0.0)
        acc = acc + _dot((u * u).astype(BF16), wdown_ref[cols, :])
    if final:
        acc = _rms(acc, fnw_ref[...])
    o_ref[...] = acc


def _mlp(x2d, norm_w, w_up, w_down, final_norm_w, final):
    m = x2d.shape[0]
    t = ROW_TILE
    row = pl.BlockSpec((t, D_MODEL), lambda i: (i, 0))
    full = lambda a: pl.BlockSpec(a.shape, lambda i: (0,) * a.ndim)
    return pl.pallas_call(
        functools.partial(_mlp_kernel, final),
        out_shape=jax.ShapeDtypeStruct((m, D_MODEL), F32),
        grid=(m // t,),
        in_specs=[row, full(norm_w), full(w_up), full(w_down), full(final_norm_w)],
        out_specs=row,
        compiler_params=_cparams(1),
        name="mlp_final" if final else "mlp",
    )(x2d, norm_w, w_up, w_down, final_norm_w)


def _dup_heads(w, n_heads, hd):
    w = w.reshape(w.shape[0], n_heads, 1, hd)
    return jnp.broadcast_to(w, (w.shape[0], n_heads, 2, hd)).reshape(w.shape[0], n_heads * 2 * hd)


def _prep_layer(l, p):
    w_in = p["w_in"][l]
    o = 0
    parts = {}
    for name, width in (("qa", SWA_Q_W), ("ka", SWA_KV_W), ("va", SWA_KV_W), ("qb", MOBA_W), ("kb", MOBA_W),
                        ("vb", MOBA_W), ("z", SSD_D_INNER), ("xbc", SSD_XBC_W), ("dt", SSD_HEADS)):
        parts[name] = w_in[:, o:o + width]
        o += width
    dt_pad = jnp.pad(parts["dt"], ((0, 0), (0, DT_PAD - SSD_HEADS)))
    w_all = jnp.concatenate([parts["qa"], _dup_heads(parts["ka"], SWA_KV_HEADS, SWA_HEAD_DIM),
                             _dup_heads(parts["va"], SWA_KV_HEADS, SWA_HEAD_DIM), parts["qb"], parts["kb"],
                             parts["z"], parts["xbc"], dt_pad], axis=1).astype(BF16)
    row = lambda v: v.reshape(1, -1).astype(F32)
    pad_row = lambda v: jnp.pad(v.astype(F32), (0, DT_PAD - SSD_HEADS)).reshape(1, DT_PAD)
    head_of_col = jnp.arange(SSD_D_INNER) // SSD_HEAD_DIM
    expand = (jnp.arange(DT_PAD)[:, None] == head_of_col[None, :]).astype(BF16)
    ssd = {
        "conv_w": p["conv_w"][l].astype(F32), "conv_b": row(p["conv_b"][l]),
        "dt_bias_row": pad_row(p["dt_bias"][l]), "dt_bias_col": p["dt_bias"][l].reshape(SSD_HEADS, 1).astype(F32),
        "a_log_row": pad_row(p["a_log"][l]), "a_log_col": p["a_log"][l].reshape(SSD_HEADS, 1).astype(F32),
        "d_skip": row(jnp.repeat(p["d_skip"][l], SSD_HEAD_DIM)), "norm_w": row(p["ssd_norm"][l]),
        "expand": expand,
    }
    return {
        "proj": {"w_all": w_all, "w_vt": parts["vb"].T.astype(BF16), "w_dtt": parts["dt"].T.astype(BF16)},
        "ssd": ssd,
        "norm_mix": row(p["norm_mix"][l]), "w_gate": p["w_gate"][l].astype(BF16), "b_gate": row(p["b_gate"][l]),
        "swa_sinks": p["swa_sinks"][l].astype(F32),
        "w_br_swa": p["w_br_swa"][l].astype(BF16), "w_br_moba": p["w_br_moba"][l].astype(BF16),
        "w_br_ssd": p["w_br_ssd"][l].astype(BF16), "w_mix_out": p["w_mix_out"][l].astype(BF16),
        "norm_cross": row(p["norm_cross"][l]), "norm_mem": row(p["norm_mem"][l]),
        "w_cq": p["w_cq"][l].astype(BF16), "w_ckv": p["w_ckv"][l].astype(BF16), "w_co": p["w_co"][l].astype(BF16),
        "norm_mlp": row(p["norm_mlp"][l]), "w_up": p["w_up"][l].astype(BF16), "w_down": p["w_down"][l].astype(BF16),
    }


def kernel(x, mem, positions, norm_mix, w_in, w_gate, b_gate, swa_sinks, conv_w, conv_b, dt_bias, a_log, d_skip,
           ssd_norm, w_br_swa, w_br_moba, w_br_ssd, w_mix_out, norm_cross, norm_mem, w_cq, w_ckv, w_co, norm_mlp,
           w_up, w_down, final_norm):
    batch, seq, d = x.shape
    depth = w_in.shape[0]
    assert d == D_MODEL and seq % MOBA_BLOCK == 0 and mem.shape[1] == MEM_LEN
    params = dict(norm_mix=norm_mix, w_in=w_in, w_gate=w_gate, b_gate=b_gate, swa_sinks=swa_sinks, conv_w=conv_w,
                  conv_b=conv_b, dt_bias=dt_bias, a_log=a_log, d_skip=d_skip, ssd_norm=ssd_norm, w_br_swa=w_br_swa,
                  w_br_moba=w_br_moba, w_br_ssd=w_br_ssd, w_mix_out=w_mix_out, norm_cross=norm_cross,
                  norm_mem=norm_mem, w_cq=w_cq, w_ckv=w_ckv, w_co=w_co, norm_mlp=norm_mlp, w_up=w_up, w_down=w_down)
    x2d = x.reshape(batch * seq, d)
    mem2d = mem.reshape(batch * MEM_LEN, d)
    fnw = final_norm.reshape(1, d).astype(F32)
    cos_t, sin_t = _rope_tables(positions)
    for l in range(depth):
        lp = _prep_layer(l, params)
        (q_a, k_a, v_a, q_b, k_b, kmean, vt_b, z, xbc, dt, dtt) = _proj_in(
            x2d, lp["norm_mix"], lp["proj"], cos_t, sin_t, batch)
        y_a = _swa(q_a, k_a, v_a, lp["swa_sinks"], batch)
        y_b = _moba(q_b, k_b, vt_b, kmean, batch)
        y_c = _ssd(z, xbc, dt, dtt, lp["ssd"], batch)
        kmem, vmem = _mem_kv(mem2d, lp["norm_mem"], lp["w_ckv"])
        x2d = _merge(x2d, y_a, y_b, y_c, kmem, vmem, lp, batch)
        x2d = _mlp(x2d, lp["norm_mlp"], lp["w_up"], lp["w_down"], fnw, final=(l == depth - 1))
    return x2d.reshape(batch, seq, d)
```

```python
import functools
import math
from typing import NamedTuple

import jax
import jax.numpy as jnp
from jax import lax
from jax.experimental import pallas as pl
from jax.experimental.pallas import tpu as pltpu

F32 = jnp.float32
BF16 = jnp.bfloat16

D_MODEL = 1024
MEM_LEN = 256
NORM_EPS = 1e-6
SWA_Q_HEADS = 8
SWA_KV_HEADS = 2
SWA_HEAD_DIM = 64
SWA_BLOCK = 128
SWA_BLOCKS_PER_STEP = 2
ROPE_THETA = 150000.0
MOBA_HEADS = 8
MOBA_HEAD_DIM = 64
MOBA_BLOCK = 256
MOBA_TOPK = 3
MOBA_GROUP = 4
MOBA_PAIRS_PER_STEP = 2
SUM_ROWS = 16
SSD_D_INNER = D_MODEL
SSD_HEAD_DIM = 64
SSD_HEADS = SSD_D_INNER // SSD_HEAD_DIM
SSD_GROUPS = 2
SSD_D_STATE = 128
SSD_CONV = 4
SSD_CHUNK = 128
X_HEADS = 4
X_HEAD_DIM = D_MODEL // X_HEADS
D_FF = 4 * D_MODEL
N_BRANCH = 3

SWA_Q_W = SWA_Q_HEADS * SWA_HEAD_DIM
SWA_KV_W = SWA_KV_HEADS * SWA_HEAD_DIM
MOBA_W = MOBA_HEADS * MOBA_HEAD_DIM
SSD_BC_W = SSD_GROUPS * SSD_D_STATE
SSD_XBC_W = SSD_D_INNER + 2 * SSD_BC_W
SSD_GROUP_W = SSD_D_INNER // SSD_GROUPS

LANES = 128
HEAD_PAIR_W = 2 * MOBA_HEAD_DIM
DT_PAD = LANES
CONV_TAIL = 16
ROW_TILE = 256
VMEM_LIMIT = 56 * 1024 * 1024

_C_QA = 0
_C_KA = _C_QA + SWA_Q_W
_C_VA = _C_KA + SWA_KV_W
_C_QB = _C_VA + SWA_KV_W
_C_KB = _C_QB + MOBA_W
_C_VB = _C_KB + MOBA_W
_C_Z = _C_VB + MOBA_W
_C_XBC = _C_Z + SSD_D_INNER
_C_DT = _C_XBC + SSD_XBC_W
D_IN_PROJ = _C_DT + SSD_HEADS

LOG2E = math.log2(math.e)
_NT = (((1,), (1,)), ((), ()))


class _LayerParam(NamedTuple):
    stacked: jax.Array
    layer: int


def _const_spec(a):
    if isinstance(a, _LayerParam):
        shape = a.stacked.shape[1:]
        layer = a.layer
        return pl.BlockSpec((None,) + shape, lambda *_: (layer,) + (0,) * len(shape))
    return pl.BlockSpec(a.shape, lambda *_: (0,) * a.ndim)


def _operands(*args):
    return tuple(a.stacked if isinstance(a, _LayerParam) else a for a in args)


def _cparams(n_axes, flags=None):
    return pltpu.CompilerParams(dimension_semantics=("arbitrary",) * n_axes,
                                vmem_limit_bytes=VMEM_LIMIT, flags=flags)


def _rms(x, w):
    ms = jnp.mean(x * x, axis=-1, keepdims=True)
    return x * lax.rsqrt(ms + NORM_EPS) * w


def _dot(a, b):
    return jnp.dot(a, b, preferred_element_type=F32)


def _dot_nt(a, b, precision=None):
    return lax.dot_general(a, b, _NT, preferred_element_type=F32, precision=precision)


def _silu(x):
    return x * jax.nn.sigmoid(x)


def _rope_table_kernel(pos_ref, inv_ref, cos_ref, sin_ref):
    ang = pos_ref[...].astype(F32) * inv_ref[...]
    lane = lax.broadcasted_iota(jnp.int32, ang.shape, 1)
    first_half = (lane % SWA_HEAD_DIM) < (SWA_HEAD_DIM // 2)
    cos_ref[...] = jnp.cos(ang)
    sin_ref[...] = jnp.where(first_half, -jnp.sin(ang), jnp.sin(ang))


def _rope_tables(positions):
    m = positions.size
    half = SWA_HEAD_DIM // 2
    inv = ROPE_THETA ** (-jnp.arange(half, dtype=F32) / half)
    inv = jnp.tile(inv, LANES // half).reshape(1, LANES)
    t = 1024 if m % 1024 == 0 else ROW_TILE
    return pl.pallas_call(
        _rope_table_kernel,
        out_shape=(jax.ShapeDtypeStruct((m, LANES), F32), jax.ShapeDtypeStruct((m, LANES), F32)),
        grid=(m // t,),
        in_specs=[pl.BlockSpec((t, 1), lambda i: (i, 0)), pl.BlockSpec((1, LANES), lambda i: (0, 0))],
        out_specs=(pl.BlockSpec((t, LANES), lambda i: (i, 0)), pl.BlockSpec((t, LANES), lambda i: (i, 0))),
        compiler_params=_cparams(1),
        name="rope_tables",
    )(positions.reshape(m, 1), inv)


def _rope(t, cos, sin):
    w = t.shape[-1]
    reps = w // LANES
    cos_w = jnp.concatenate([cos] * reps, axis=1)
    sin_w = jnp.concatenate([sin] * reps, axis=1)
    half = SWA_HEAD_DIM // 2
    lane = lax.broadcasted_iota(jnp.int32, t.shape, 1)
    first_half = (lane % SWA_HEAD_DIM) < half
    partner = jnp.where(first_half, pltpu.roll(t, w - half, 1), pltpu.roll(t, half, 1))
    return t * cos_w + partner * sin_w


def _dup_kv_heads(t):
    swapped = pltpu.roll(t, SWA_HEAD_DIM, 1)
    low = lax.broadcasted_iota(jnp.int32, t.shape, 1) < SWA_HEAD_DIM
    return jnp.concatenate([jnp.where(low, t, swapped), jnp.where(low, swapped, t)], axis=1)


def _proj_in_kernel(nkb, x_ref, nw_ref, w_ref, wvt_ref, wdt_ref, wdtt_ref, cos_ref, sin_ref,
                    qa_ref, ka_ref, va_ref, qb_ref, kb_ref, kmean_ref, vt_ref,
                    z_ref, xbc_ref, dt_ref, dtt_ref):
    i = pl.program_id(0)
    h = _rms(x_ref[...], nw_ref[...]).astype(BF16)
    cos = cos_ref[...]
    sin = sin_ref[...]

    def seg(a, b):
        return _dot(h, w_ref[:, a:b])

    qa_ref[...] = _rope(seg(_C_QA, _C_KA), cos, sin).astype(BF16)
    ka_ref[...] = _dup_kv_heads(_rope(seg(_C_KA, _C_VA), cos, sin)).astype(BF16)
    va_ref[...] = _dup_kv_heads(seg(_C_VA, _C_QB)).astype(BF16)
    qb_ref[...] = seg(_C_QB, _C_KB).astype(BF16)
    kb = seg(_C_KB, _C_VB)
    kb_ref[...] = (kb * LOG2E).astype(BF16)
    kmean_ref[0, pl.ds(i % nkb, 1), :] = jnp.mean(kb, axis=0, keepdims=True)
    vt_ref[...] = _dot_nt(wvt_ref[...], h).astype(BF16)
    z_ref[...] = seg(_C_Z, _C_XBC).astype(BF16)
    xbc_ref[...] = seg(_C_XBC, _C_DT).astype(BF16)
    dt_ref[...] = _dot(h, wdt_ref[...])
    dtt_ref[...] = _dot_nt(wdtt_ref[...], h)


def _proj_in(x2d, norm_w, w_in, w_vt, w_dt, w_dtt, cos_t, sin_t, batch):
    m = x2d.shape[0]
    t = ROW_TILE
    n = m // t
    nkb = n // batch
    row = lambda w: pl.BlockSpec((t, w), lambda i: (i, 0))
    full = _const_spec
    out_shape = (
        jax.ShapeDtypeStruct((m, SWA_Q_W), BF16),
        jax.ShapeDtypeStruct((m, 2 * SWA_KV_W), BF16),
        jax.ShapeDtypeStruct((m, 2 * SWA_KV_W), BF16),
        jax.ShapeDtypeStruct((m, MOBA_W), BF16),
        jax.ShapeDtypeStruct((m, MOBA_W), BF16),
        jax.ShapeDtypeStruct((batch, nkb, MOBA_W), F32),
        jax.ShapeDtypeStruct((MOBA_W, m), BF16),
        jax.ShapeDtypeStruct((m, SSD_D_INNER), BF16),
        jax.ShapeDtypeStruct((m, SSD_XBC_W), BF16),
        jax.ShapeDtypeStruct((m, DT_PAD), F32),
        jax.ShapeDtypeStruct((SSD_HEADS, m), F32),
    )
    out_specs = (
        row(SWA_Q_W), row(2 * SWA_KV_W), row(2 * SWA_KV_W), row(MOBA_W), row(MOBA_W),
        pl.BlockSpec((1, nkb, MOBA_W), lambda i: (i // nkb, 0, 0)),
        pl.BlockSpec((MOBA_W, t), lambda i: (0, i)),
        row(SSD_D_INNER), row(SSD_XBC_W), row(DT_PAD),
        pl.BlockSpec((SSD_HEADS, t), lambda i: (0, i)),
    )
    return pl.pallas_call(
        functools.partial(_proj_in_kernel, nkb),
        out_shape=out_shape,
        grid=(n,),
        in_specs=[row(D_MODEL), full(norm_w), full(w_in), full(w_vt), full(w_dt), full(w_dtt),
                  row(LANES), row(LANES)],
        out_specs=out_specs,
        compiler_params=_cparams(1),
        name="proj_in",
    )(*_operands(x2d, norm_w, w_in, w_vt, w_dt, w_dtt, cos_t, sin_t))


def _swa_kernel(sink_ref, q_ref, kp_ref, kc_ref, vp_ref, vc_ref, o_ref):
    i = pl.program_id(1)
    blk = SWA_BLOCK
    kall = jnp.concatenate([kp_ref[...], kc_ref[...]], axis=0)
    vall = jnp.concatenate([vp_ref[...], vc_ref[...]], axis=0)
    qi = lax.broadcasted_iota(jnp.int32, (blk, 2 * blk), 0)
    si = lax.broadcasted_iota(jnp.int32, (blk, 2 * blk), 1)
    delta = qi + blk - si
    in_window = (delta >= 0) & (delta < blk)
    lane = lax.broadcasted_iota(jnp.int32, (blk, LANES), 1)
    low = lane < SWA_HEAD_DIM
    group = SWA_Q_HEADS // SWA_KV_HEADS
    for sub in range(SWA_BLOCKS_PER_STEP):
        rows = slice(sub * blk, (sub + 1) * blk)
        mask = in_window & ((si >= blk) | (i > 0)) if sub == 0 else in_window
        q = q_ref[rows, :] * (SWA_HEAD_DIM ** -0.5)
        kcat = kall[sub * blk:(sub + 2) * blk]
        vcat = vall[sub * blk:(sub + 2) * blk]
        outs = []
        for hd in range(SWA_Q_HEADS):
            g = hd // group
            kd = kcat[:, g * LANES:(g + 1) * LANES]
            vd = vcat[:, g * LANES:(g + 1) * LANES]
            slab = q[:, (hd // 2) * LANES:(hd // 2 + 1) * LANES]
            qm = jnp.where(low if hd % 2 == 0 else ~low, slab, jnp.zeros_like(slab))
            s = jnp.where(mask, _dot_nt(qm, kd), -jnp.inf)
            sink = sink_ref[hd]
            mx = jnp.maximum(jnp.max(s, axis=-1, keepdims=True), sink)
            p = jnp.exp(s - mx)
            denom = jnp.sum(p, axis=-1, keepdims=True) + jnp.exp(sink - mx)
            outs.append(_dot(p.astype(BF16), vd) / denom)
        for pr in range(SWA_Q_HEADS // 2):
            o_ref[rows, pr * LANES:(pr + 1) * LANES] = jnp.where(low, outs[2 * pr], outs[2 * pr + 1]).astype(BF16)


def _swa(q_a, k_a, v_a, sinks, batch):
    m = q_a.shape[0]
    blk = SWA_BLOCK
    per = SWA_BLOCKS_PER_STEP
    nb = m // batch // blk
    steps = nb // per
    kvw = 2 * SWA_KV_W
    cur = lambda b, i: (b * steps + i, 0)
    prev = lambda b, i: (b * nb + jnp.maximum(per * i - 1, 0), 0)
    return pl.pallas_call(
        _swa_kernel,
        out_shape=jax.ShapeDtypeStruct((m, SWA_Q_W), BF16),
        grid=(batch, steps),
        in_specs=[pl.BlockSpec(memory_space=pltpu.SMEM),
                  pl.BlockSpec((per * blk, SWA_Q_W), cur),
                  pl.BlockSpec((blk, kvw), prev), pl.BlockSpec((per * blk, kvw), cur),
                  pl.BlockSpec((blk, kvw), prev), pl.BlockSpec((per * blk, kvw), cur)],
        out_specs=pl.BlockSpec((per * blk, SWA_Q_W), cur),
        compiler_params=_cparams(2),
        name="swa",
    )(sinks, q_a, k_a, k_a, v_a, v_a)


def _fold8(x, op):
    return functools.reduce(op, [x[r:r + 8] for r in range(0, x.shape[0], 8)])


def _moba_kernel(q_ref, k_ref, vt_ref, kmean_ref, o_ref, *scratch):
    i = pl.program_id(2)
    per_pipe = len(scratch) // MOBA_PAIRS_PER_STEP
    pipes = []
    for ps in range(MOBA_PAIRS_PER_STEP):
        lanes = pl.ds(ps * HEAD_PAIR_W, HEAD_PAIR_W)
        pipes.append(_moba_pipeline(i, q_ref.at[:, lanes], k_ref.at[:, lanes], vt_ref.at[lanes, :],
                                    kmean_ref.at[:, :, lanes], o_ref.at[:, lanes],
                                    *scratch[ps * per_pipe:(ps + 1) * per_pipe]))
    n_groups = (i + MOBA_GROUP - 1) // MOBA_GROUP

    for p in pipes:
        p["setup"]()
    for p in pipes:
        p["scores_own"]()
    mx0 = [None] * len(pipes)
    for r in range(MOBA_GROUP):
        for n, p in enumerate(pipes):
            mx0[n] = p["score_block"](0, 0, r, mx0[n])
            if r == 0:
                p["accumulate_own"]()
    for n, p in enumerate(pipes):
        p["end_scores"](0, mx0[n])

    def stage(score_g, score_buf, acc_g, acc_buf):
        state = [p["begin"](acc_buf) for p in pipes]
        mx = [None] * len(pipes)
        pv = [None] * len(pipes)
        for r in range(MOBA_GROUP):
            for n, p in enumerate(pipes):
                mx[n] = p["score_block"](score_g, score_buf, r, mx[n])
                pv[n] = p["exp_block"](acc_g, acc_buf, r, state[n], pv[n])
        for n, p in enumerate(pipes):
            p["end_scores"](score_buf, mx[n])
            p["finish"](state[n], pv[n])

    def pair(u, carry):
        g = 2 * u
        stage(g + 1, 1, g, 0)
        stage(g + 2, 0, g + 1, 1)
        return carry

    lax.fori_loop(0, n_groups // 2, pair, 0)

    @pl.when(n_groups % 2 == 1)
    def _():
        for p in pipes:
            p["accumulate"](n_groups - 1, 0)

    for p in pipes:
        p["finalize"]()


def _moba_pipeline(i, q_ref, k_ref, vt_ref, kmean_ref, o_ref, qm_ref, sel_ref, ml_ref, acc_ref,
                   so_ref, mxo_ref, po_ref, sa_ref, mxa_ref, pa_ref, sb_ref, mxb_ref, pb_ref):
    blk = MOBA_BLOCK
    hd = MOBA_HEAD_DIM
    grp = MOBA_GROUP
    nkb = kmean_ref.shape[1]
    stage_bufs = ((sa_ref, mxa_ref, pa_ref), (sb_ref, mxb_ref, pb_ref))

    def setup():
        q = q_ref[...] * (hd ** -0.5)
        lane = lax.broadcasted_iota(jnp.int32, q.shape, 1)
        qm_ref[0:blk, :] = jnp.where(lane < hd, q, jnp.zeros_like(q))
        qm_ref[blk:2 * blk, :] = jnp.where(lane >= hd, q, jnp.zeros_like(q))

        blk_id = lax.broadcasted_iota(jnp.int32, (nkb, 2 * blk), 0)
        gate = None
        resid = kmean_ref[0]
        for _ in range(3):
            part = resid.astype(BF16)
            resid = resid - part.astype(F32)
            term = _dot_nt(part, qm_ref[...])
            gate = term if gate is None else gate + term
        gate = jnp.where(blk_id < i, gate, -jnp.inf)
        sel = jnp.zeros((nkb, 2 * blk), F32)
        for _ in range(min(MOBA_TOPK, nkb)):
            best = jnp.max(gate, axis=0, keepdims=True)
            idx = jnp.min(jnp.where(gate == best, blk_id, nkb), axis=0, keepdims=True)
            pick = blk_id == idx
            sel = jnp.where(pick & (blk_id < i), 1.0, sel)
            gate = jnp.where(pick, -jnp.inf, gate)
        sel_ref[0:nkb, :] = sel
        sel_ref[nkb:nkb + grp, :] = jnp.zeros((grp, 2 * blk), F32)

    def score_block(start, r, keep, s_buf, mx):
        s = _dot_nt(k_ref[pl.ds(start + r * blk, blk), :], qm_ref[...])
        s = jnp.where(keep, s, -jnp.inf)
        s_buf[r * blk:(r + 1) * blk, :] = s
        m8 = _fold8(s, jnp.maximum)
        return m8 if mx is None else jnp.maximum(mx, m8)

    def begin(mx_buf, first):
        mx = jnp.max(mx_buf[...], axis=0, keepdims=True)
        if first:
            m_new, alpha = mx, None
        else:
            m_old = ml_ref[0:1, :]
            m_new = jnp.maximum(m_old, mx)
            alpha = jnp.exp2(m_old - m_new)
        ml_ref[0:1, :] = m_new
        return m_new, alpha

    def exp_block(s_buf, p_buf, r, m_new):
        p_buf[r * blk:(r + 1) * blk, :] = jnp.exp2(s_buf[r * blk:(r + 1) * blk, :] - m_new).astype(BF16)

    def pv_block(start, r, p_buf, pv):
        ones = jnp.ones((SUM_ROWS, blk), BF16)
        out = []
        for h in range(2):
            vt_ext = jnp.concatenate([vt_ref[h * hd:(h + 1) * hd, pl.ds(start + r * blk, blk)], ones], axis=0)
            term = _dot(vt_ext, p_buf[r * blk:(r + 1) * blk, h * blk:(h + 1) * blk])
            out.append(term if pv is None else pv[h] + term)
        return out

    def finish(pv, alpha):
        for h in range(2):
            acc_ref[h] = pv[h] if alpha is None else alpha[:, h * blk:(h + 1) * blk] * acc_ref[h] + pv[h]

    def group_start(g):
        return pl.multiple_of(jnp.minimum(g, nkb // grp - 1) * (grp * blk), grp * blk)

    def group_score_block(g, buf, r, mx):
        keep = sel_ref[pl.ds(g * grp + r, 1), :] > 0.0
        return score_block(group_start(g), r, keep, stage_bufs[buf][0], mx)

    def group_end_scores(buf, mx):
        stage_bufs[buf][1][...] = mx

    def group_scores(g, buf):
        mx = None
        for r in range(grp):
            mx = group_score_block(g, buf, r, mx)
        group_end_scores(buf, mx)

    def group_begin(buf):
        return begin(stage_bufs[buf][1], False)

    def group_exp_block(g, buf, r, state, pv):
        exp_block(stage_bufs[buf][0], stage_bufs[buf][2], r, state[0])
        return pv_block(group_start(g), r, stage_bufs[buf][2], pv)

    def group_finish(state, pv):
        finish(pv, state[1])

    def group_accumulate(g, buf):
        state = group_begin(buf)
        pv = None
        for r in range(grp):
            pv = group_exp_block(g, buf, r, state, pv)
        group_finish(state, pv)

    own = pl.multiple_of(i * blk, blk)

    def scores_own():
        kpos = lax.broadcasted_iota(jnp.int32, (blk, 2 * blk), 0)
        qpos = lax.broadcasted_iota(jnp.int32, (blk, 2 * blk), 1) % blk
        mxo_ref[...] = score_block(own, 0, kpos <= qpos, so_ref, None)

    def accumulate_own():
        m_new, alpha = begin(mxo_ref, True)
        exp_block(so_ref, po_ref, 0, m_new)
        finish(pv_block(own, 0, po_ref, None), alpha)

    def finalize():
        out_t = jnp.concatenate([acc_ref[h, 0:hd, :] / acc_ref[h, hd:hd + 1, :] for h in range(2)], axis=0)
        o_ref[...] = out_t.T.astype(BF16)

    return dict(setup=setup, scores_own=scores_own, scores=group_scores, accumulate_own=accumulate_own,
                accumulate=group_accumulate, finalize=finalize, begin=group_begin,
                score_block=group_score_block, exp_block=group_exp_block, end_scores=group_end_scores,
                finish=group_finish)


def _moba_stage(n_blocks, blk):
    return [pltpu.VMEM((n_blocks * blk, 2 * blk), F32), pltpu.VMEM((8, 2 * blk), F32),
            pltpu.VMEM((n_blocks * blk, 2 * blk), BF16)]


def _moba(q_b, k_b, vt_b, kmean, batch):
    m = q_b.shape[0]
    s = m // batch
    blk = MOBA_BLOCK
    nkb = s // blk
    assert nkb % MOBA_GROUP == 0
    w = MOBA_PAIRS_PER_STEP * HEAD_PAIR_W
    steps = MOBA_W // w
    pipe_scratch = [pltpu.VMEM((2 * blk, HEAD_PAIR_W), BF16),
                    pltpu.VMEM((nkb + MOBA_GROUP, 2 * blk), F32),
                    pltpu.VMEM((8, 2 * blk), F32),
                    pltpu.VMEM((2, MOBA_HEAD_DIM + SUM_ROWS, blk), F32),
                    ] + _moba_stage(1, blk) + _moba_stage(MOBA_GROUP, blk) + _moba_stage(MOBA_GROUP, blk)
    return pl.pallas_call(
        _moba_kernel,
        out_shape=jax.ShapeDtypeStruct((m, MOBA_W), BF16),
        grid=(batch, steps, nkb),
        in_specs=[pl.BlockSpec((blk, w), lambda b, p, i: (b * nkb + i, p)),
                  pl.BlockSpec((s, w), lambda b, p, i: (b, p)),
                  pl.BlockSpec((w, s), lambda b, p, i: (p, b)),
                  pl.BlockSpec((1, nkb, w), lambda b, p, i: (b, 0, p))],
        out_specs=pl.BlockSpec((blk, w), lambda b, p, i: (b * nkb + i, p)),
        scratch_shapes=pipe_scratch * MOBA_PAIRS_PER_STEP,
        compiler_params=_cparams(3),
        name="moba",
    )(q_b, k_b, vt_b, kmean)


def _split_dot(v, e):
    hi = v.astype(BF16)
    lo = (v - hi.astype(F32)).astype(BF16)
    return _dot(hi, e) + _dot(lo, e)


def _ssd_kernel(z_ref, xbc_ref, dt_ref, dtt_ref, cw_ref, cb_ref, dtb_ref, dtbc_ref, alog_ref, alogc_ref,
                dskip_ref, nw_ref, exp_ref, shift_ref, o_ref, ext_ref, state_ref):
    c = pl.program_id(1)
    L = SSD_CHUNK
    tail = CONV_TAIL

    @pl.when(c == 0)
    def _():
        ext_ref[0:tail, :] = jnp.zeros((tail, SSD_XBC_W), BF16)
        state_ref[...] = jnp.zeros_like(state_ref)

    ext_ref[tail:tail + L, :] = xbc_ref[...]
    shifted = _dot(shift_ref[...], ext_ref[...])
    acc = cb_ref[...] + cw_ref[SSD_CONV - 1:SSD_CONV, :] * xbc_ref[...].astype(F32)
    for j in range(SSD_CONV - 1):
        acc = acc + cw_ref[j:j + 1, :] * shifted[j * L:(j + 1) * L, :]
    ext_ref[0:tail, :] = ext_ref[L:L + tail, :]
    xc = _silu(acc)
    xs = xc[:, :SSD_D_INNER]
    bm = xc[:, SSD_D_INNER:SSD_D_INNER + SSD_BC_W]
    cm = xc[:, SSD_D_INNER + SSD_BC_W:]

    dt = jax.nn.softplus(dt_ref[...] + dtb_ref[...])
    dtt = jax.nn.softplus(dtt_ref[...] + dtbc_ref[...])
    a_row = -jnp.exp(alog_ref[...])
    a_col = -jnp.exp(alogc_ref[...])
    r_i = lax.broadcasted_iota(jnp.int32, (L, L), 0)
    c_i = lax.broadcasted_iota(jnp.int32, (L, L), 1)
    tril = r_i >= c_i
    hp = lax.Precision.HIGHEST
    a_cs = jnp.dot(tril.astype(F32), dt * a_row, precision=hp, preferred_element_type=F32)
    a_cst = jnp.dot(dtt * a_col, (r_i <= c_i).astype(F32), precision=hp, preferred_element_type=F32)
    a_last = a_cs[L - 1:L, :]
    expand = exp_ref[...]
    decay_out = _split_dot(jnp.exp(a_cs), expand)
    decay_in = _split_dot(dt * jnp.exp(a_last - a_cs), expand)
    chunk_decay = decay_out[L - 1:L, :]

    lane = lax.broadcasted_iota(jnp.int32, (L, LANES), 1)
    low = lane < SSD_HEAD_DIM
    heads_per_group = SSD_HEADS // SSD_GROUPS
    ys = []
    for g in range(SSD_GROUPS):
        b_g = bm[:, g * SSD_D_STATE:(g + 1) * SSD_D_STATE]
        c_g = cm[:, g * SSD_D_STATE:(g + 1) * SSD_D_STATE]
        cb = _dot_nt(c_g.astype(BF16), b_g.astype(BF16))
        gcols = slice(g * SSD_GROUP_W, (g + 1) * SSD_GROUP_W)
        y_off = _dot(c_g.astype(BF16), state_ref[g].astype(BF16)) * decay_out[:, gcols]
        y_diag = []
        for pr in range(heads_per_group // 2):
            col0 = g * SSD_GROUP_W + pr * LANES
            x_pair = xs[:, col0:col0 + LANES].astype(BF16)
            halves = []
            for e in range(2):
                hidx = g * heads_per_group + 2 * pr + e
                diff = a_cs[:, hidx:hidx + 1] - a_cst[hidx:hidx + 1, :]
                wmat = cb * jnp.exp(jnp.where(tril, diff, -jnp.inf)) * dtt[hidx:hidx + 1, :]
                halves.append(_dot(wmat.astype(BF16), x_pair))
            y_diag.append(jnp.where(low, halves[0], halves[1]))
        ys.append(jnp.concatenate(y_diag, axis=1) + y_off)
        xw = (xs[:, gcols] * decay_in[:, gcols]).astype(BF16)
        state_ref[g] = state_ref[g] * chunk_decay[:, gcols] + _dot(b_g.T.astype(BF16), xw)

    y = jnp.concatenate(ys, axis=1) + xs * dskip_ref[...]
    y = y * _silu(z_ref[...].astype(F32))
    outs = []
    for g in range(SSD_GROUPS):
        yg = y[:, g * SSD_GROUP_W:(g + 1) * SSD_GROUP_W]
        outs.append(yg * lax.rsqrt(jnp.mean(yg * yg, axis=-1, keepdims=True) + NORM_EPS))
    o_ref[...] = (jnp.concatenate(outs, axis=1) * nw_ref[...]).astype(BF16)


def _ssd(z, xbc, dt, dtt, sp, batch):
    m = z.shape[0]
    L = SSD_CHUNK
    nc = m // batch // L
    row = lambda w: pl.BlockSpec((L, w), lambda b, c: (b * nc + c, 0))
    full = _const_spec
    consts = (sp["conv_w"], sp["conv_b"], sp["dt_bias_row"], sp["dt_bias_col"], sp["a_log_row"],
              sp["a_log_col"], sp["d_skip"], sp["norm_w"], sp["expand"], sp["shift"])
    return pl.pallas_call(
        _ssd_kernel,
        out_shape=jax.ShapeDtypeStruct((m, SSD_D_INNER), BF16),
        grid=(batch, nc),
        in_specs=[row(SSD_D_INNER), row(SSD_XBC_W), row(DT_PAD),
                  pl.BlockSpec((SSD_HEADS, L), lambda b, c: (0, b * nc + c))] + [full(a) for a in consts],
        out_specs=row(SSD_D_INNER),
        scratch_shapes=[pltpu.VMEM((L + CONV_TAIL, SSD_XBC_W), BF16),
                        pltpu.VMEM((SSD_GROUPS, SSD_D_STATE, SSD_GROUP_W), F32)],
        compiler_params=_cparams(2),
        name="ssd",
    )(*_operands(z, xbc, dt, dtt, *consts))


def _mem_kv_kernel(mem_ref, nw_ref, w_ref, k_ref, v_ref):
    hm = _rms(mem_ref[...], nw_ref[...]).astype(BF16)
    k_ref[...] = _dot(hm, w_ref[:, :D_MODEL]).astype(BF16)
    v_ref[...] = _dot(hm, w_ref[:, D_MODEL:]).astype(BF16)


def _mem_kv(mem2d, norm_w, w_ckv):
    m = mem2d.shape[0]
    t = MEM_LEN
    row = pl.BlockSpec((t, D_MODEL), lambda i: (i, 0))
    full = _const_spec
    return pl.pallas_call(
        _mem_kv_kernel,
        out_shape=(jax.ShapeDtypeStruct((m, D_MODEL), BF16), jax.ShapeDtypeStruct((m, D_MODEL), BF16)),
        grid=(m // t,),
        in_specs=[row, full(norm_w), full(w_ckv)],
        out_specs=(row, row),
        compiler_params=_cparams(1),
        name="mem_kv",
    )(*_operands(mem2d, norm_w, w_ckv))


def _merge_kernel(x_ref, ya_ref, yb_ref, yc_ref, nmix_ref, wg_ref, bg_ref, wa_ref, wb_ref, wc_ref, wmix_ref,
                  ncross_ref, wcq_ref, kmem_ref, vmem_ref, wco_ref, o_ref):
    x = x_ref[...]
    h = _rms(x, nmix_ref[...]).astype(BF16)
    merged = None
    for br, (y_ref, w_ref) in enumerate(((ya_ref, wa_ref), (yb_ref, wb_ref), (yc_ref, wc_ref))):
        cols = slice(br * D_MODEL, (br + 1) * D_MODEL)
        gate = jax.nn.sigmoid(_dot(h, wg_ref[:, cols]) + bg_ref[:, cols])
        term = gate * _dot(y_ref[...], w_ref[...])
        merged = term if merged is None else merged + term
    x = x + _dot(merged.astype(BF16), wmix_ref[...])

    hq = _rms(x, ncross_ref[...]).astype(BF16)
    q = _dot(hq, wcq_ref[...]).astype(BF16)
    scale = X_HEAD_DIM ** -0.5
    outs = []
    for hh in range(X_HEADS):
        cols = slice(hh * X_HEAD_DIM, (hh + 1) * X_HEAD_DIM)
        s = _dot_nt(q[:, cols], kmem_ref[:, cols]) * scale
        p = jnp.exp(s - jnp.max(s, axis=-1, keepdims=True))
        denom = jnp.sum(p, axis=-1, keepdims=True)
        outs.append(_dot(p.astype(BF16), vmem_ref[:, cols]) / denom)
    o = jnp.concatenate(outs, axis=1).astype(BF16)
    o_ref[...] = x + _dot(o, wco_ref[...])


def _merge(x2d, y_a, y_b, y_c, kmem, vmem, lp, batch):
    m = x2d.shape[0]
    t = ROW_TILE
    n = m // t
    per_b = n // batch
    row = lambda w: pl.BlockSpec((t, w), lambda i: (i, 0))
    full = _const_spec
    memspec = pl.BlockSpec((MEM_LEN, D_MODEL), lambda i: (i // per_b, 0))
    return pl.pallas_call(
        _merge_kernel,
        out_shape=jax.ShapeDtypeStruct((m, D_MODEL), F32),
        grid=(n,),
        in_specs=[row(D_MODEL), row(SWA_Q_W), row(MOBA_W), row(SSD_D_INNER),
                  full(lp["norm_mix"]), full(lp["w_gate"]), full(lp["b_gate"]),
                  full(lp["w_br_swa"]), full(lp["w_br_moba"]), full(lp["w_br_ssd"]), full(lp["w_mix_out"]),
                  full(lp["norm_cross"]), full(lp["w_cq"]), memspec, memspec, full(lp["w_co"])],
        out_specs=row(D_MODEL),
        compiler_params=_cparams(1),
        name="merge_cross",
    )(*_operands(x2d, y_a, y_b, y_c, lp["norm_mix"], lp["w_gate"], lp["b_gate"], lp["w_br_swa"], lp["w_br_moba"],
                 lp["w_br_ssd"], lp["w_mix_out"], lp["norm_cross"], lp["w_cq"], kmem, vmem, lp["w_co"]))


def _mlp_kernel(final, x_ref, nw_ref, wup_ref, wdown_ref, fnw_ref, o_ref):
    x = x_ref[...]
    hf = _rms(x, nw_ref[...]).astype(BF16)
    chunk = D_MODEL
    acc = x
    for c in range(D_FF // chunk):
        cols = slice(c * chunk, (c + 1) * chunk)
        u = jnp.maximum(_dot(hf, wup_ref[:, cols]), 0.0)
        acc = acc + _dot((u * u).astype(BF16), wdown_ref[cols, :])
    if final:
        acc = _rms(acc, fnw_ref[...])
    o_ref[...] = acc


def _mlp(x2d, norm_w, w_up, w_down, final_norm_w, final):
    m = x2d.shape[0]
    t = ROW_TILE
    row = pl.BlockSpec((t, D_MODEL), lambda i: (i, 0))
    full = _const_spec
    return pl.pallas_call(
        functools.partial(_mlp_kernel, final),
        out_shape=jax.ShapeDtypeStruct((m, D_MODEL), F32),
        grid=(m // t,),
        in_specs=[row, full(norm_w), full(w_up), full(w_down), full(final_norm_w)],
        out_specs=row,
        compiler_params=_cparams(1),
        name="mlp_final" if final else "mlp",
    )(*_operands(x2d, norm_w, w_up, w_down, final_norm_w))


_MATMUL_WEIGHTS = ("w_in", "w_gate", "w_br_swa", "w_br_moba", "w_br_ssd", "w_mix_out", "w_cq", "w_ckv", "w_co",
                   "w_up", "w_down")
_ROW_PARAMS = ("norm_mix", "b_gate", "norm_cross", "norm_mem", "norm_mlp", "conv_b", "ssd_norm")


def _prep_params(p):
    depth = p["w_in"].shape[0]
    out = {k: p[k].astype(BF16) for k in _MATMUL_WEIGHTS}
    out.update({k: p[k].astype(F32).reshape(depth, 1, -1) for k in _ROW_PARAMS})
    out["w_dt"] = jnp.pad(out["w_in"][:, :, _C_DT:], ((0, 0), (0, 0), (0, DT_PAD - SSD_HEADS)))
    out["w_vt"] = jnp.swapaxes(out["w_in"][:, :, _C_VB:_C_Z], 1, 2)
    out["w_dtt"] = jnp.swapaxes(out["w_in"][:, :, _C_DT:], 1, 2)
    pad_row = lambda v: jnp.pad(v.astype(F32), ((0, 0), (0, DT_PAD - SSD_HEADS))).reshape(depth, 1, DT_PAD)
    col = lambda v: v.astype(F32).reshape(depth, SSD_HEADS, 1)
    out.update(conv_w=p["conv_w"].astype(F32), dt_bias_row=pad_row(p["dt_bias"]), dt_bias_col=col(p["dt_bias"]),
               a_log_row=pad_row(p["a_log"]), a_log_col=col(p["a_log"]),
               d_skip=jnp.repeat(p["d_skip"].astype(F32), SSD_HEAD_DIM, axis=1).reshape(depth, 1, SSD_D_INNER),
               swa_sinks=p["swa_sinks"].astype(F32))
    return out


def _ssd_constants():
    head_of_col = jnp.arange(SSD_D_INNER) // SSD_HEAD_DIM
    expand = (jnp.arange(DT_PAD)[:, None] == head_of_col[None, :]).astype(BF16)
    t_idx = jnp.arange(SSD_CHUNK)[None, :, None]
    j_idx = jnp.arange(SSD_CONV - 1)[:, None, None]
    col = jnp.arange(SSD_CHUNK + CONV_TAIL)[None, None, :]
    shift = (col == CONV_TAIL + t_idx - (SSD_CONV - 1) + j_idx).astype(BF16)
    return expand, shift.reshape((SSD_CONV - 1) * SSD_CHUNK, SSD_CHUNK + CONV_TAIL)


def _layer_params(l, sp, expand, shift):
    lp = {k: _LayerParam(v, l) for k, v in sp.items() if k != "swa_sinks"}
    lp["swa_sinks"] = sp["swa_sinks"][l]
    lp["ssd"] = {"conv_w": lp["conv_w"], "conv_b": lp["conv_b"], "dt_bias_row": lp["dt_bias_row"],
                 "dt_bias_col": lp["dt_bias_col"], "a_log_row": lp["a_log_row"], "a_log_col": lp["a_log_col"],
                 "d_skip": lp["d_skip"], "norm_w": lp["ssd_norm"], "expand": expand, "shift": shift}
    return lp


def kernel(x, mem, positions, norm_mix, w_in, w_gate, b_gate, swa_sinks, conv_w, conv_b, dt_bias, a_log, d_skip,
           ssd_norm, w_br_swa, w_br_moba, w_br_ssd, w_mix_out, norm_cross, norm_mem, w_cq, w_ckv, w_co, norm_mlp,
           w_up, w_down, final_norm):
    batch, seq, d = x.shape
    depth = w_in.shape[0]
    assert d == D_MODEL and seq % MOBA_BLOCK == 0 and mem.shape[1] == MEM_LEN
    params = dict(norm_mix=norm_mix, w_in=w_in, w_gate=w_gate, b_gate=b_gate, swa_sinks=swa_sinks, conv_w=conv_w,
                  conv_b=conv_b, dt_bias=dt_bias, a_log=a_log, d_skip=d_skip, ssd_norm=ssd_norm, w_br_swa=w_br_swa,
                  w_br_moba=w_br_moba, w_br_ssd=w_br_ssd, w_mix_out=w_mix_out, norm_cross=norm_cross,
                  norm_mem=norm_mem, w_cq=w_cq, w_ckv=w_ckv, w_co=w_co, norm_mlp=norm_mlp, w_up=w_up, w_down=w_down)
    x2d = x.reshape(batch * seq, d)
    mem2d = mem.reshape(batch * MEM_LEN, d)
    fnw = final_norm.reshape(1, d).astype(F32)
    cos_t, sin_t = _rope_tables(positions)
    stacked = _prep_params(params)
    expand, shift = _ssd_constants()
    for l in range(depth):
        lp = _layer_params(l, stacked, expand, shift)
        (q_a, k_a, v_a, q_b, k_b, kmean, vt_b, z, xbc, dt, dtt) = _proj_in(
            x2d, lp["norm_mix"], lp["w_in"], lp["w_vt"], lp["w_dt"], lp["w_dtt"], cos_t, sin_t, batch)
        y_a = _swa(q_a, k_a, v_a, lp["swa_sinks"], batch)
        y_b = _moba(q_b, k_b, vt_b, kmean, batch)
        y_c = _ssd(z, xbc, dt, dtt, lp["ssd"], batch)
        kmem, vmem = _mem_kv(mem2d, lp["norm_mem"], lp["w_ckv"])
        x2d = _merge(x2d, y_a, y_b, y_c, kmem, vmem, lp, batch)
        x2d = _mlp(x2d, lp["norm_mlp"], lp["w_up"], lp["w_down"], fnw, final=(l == depth - 1))
    return x2d.reshape(batch, seq, d)
```

```python
import functools
import math
from typing import NamedTuple

import jax
import jax.numpy as jnp
from jax import lax
from jax.experimental import pallas as pl
from jax.experimental.pallas import tpu as pltpu

F32 = jnp.float32
BF16 = jnp.bfloat16

D_MODEL = 1024
MEM_LEN = 256
NORM_EPS = 1e-6
SWA_Q_HEADS = 8
SWA_KV_HEADS = 2
SWA_HEAD_DIM = 64
SWA_BLOCK = 128
SWA_BLOCKS_PER_STEP = 2
ROPE_THETA = 150000.0
MOBA_HEADS = 8
MOBA_HEAD_DIM = 64
MOBA_BLOCK = 256
MOBA_TOPK = 3
MOBA_GROUP = 4
MOBA_PAIRS_PER_STEP = 2
SUM_ROWS = 16
SSD_D_INNER = D_MODEL
SSD_HEAD_DIM = 64
SSD_HEADS = SSD_D_INNER // SSD_HEAD_DIM
SSD_GROUPS = 2
SSD_D_STATE = 128
SSD_CONV = 4
SSD_CHUNK = 128
SSD_CHUNKS_PER_STEP = 2
X_HEADS = 4
X_HEAD_DIM = D_MODEL // X_HEADS
D_FF = 4 * D_MODEL
N_BRANCH = 3

SWA_Q_W = SWA_Q_HEADS * SWA_HEAD_DIM
SWA_KV_W = SWA_KV_HEADS * SWA_HEAD_DIM
MOBA_W = MOBA_HEADS * MOBA_HEAD_DIM
SSD_BC_W = SSD_GROUPS * SSD_D_STATE
SSD_XBC_W = SSD_D_INNER + 2 * SSD_BC_W
SSD_GROUP_W = SSD_D_INNER // SSD_GROUPS

LANES = 128
HEAD_PAIR_W = 2 * MOBA_HEAD_DIM
DT_PAD = LANES
CONV_TAIL = 16
ROW_TILE = 256
MLP_ROW_TILE = 512
VMEM_LIMIT = 56 * 1024 * 1024

_C_QA = 0
_C_KA = _C_QA + SWA_Q_W
_C_VA = _C_KA + SWA_KV_W
_C_QB = _C_VA + SWA_KV_W
_C_KB = _C_QB + MOBA_W
_C_VB = _C_KB + MOBA_W
_C_Z = _C_VB + MOBA_W
_C_XBC = _C_Z + SSD_D_INNER
_C_DT = _C_XBC + SSD_XBC_W
D_IN_PROJ = _C_DT + SSD_HEADS

LOG2E = math.log2(math.e)
_NT = (((1,), (1,)), ((), ()))


class _LayerParam(NamedTuple):
    stacked: jax.Array
    layer: int


def _const_spec(a):
    if isinstance(a, _LayerParam):
        shape = a.stacked.shape[1:]
        layer = a.layer
        return pl.BlockSpec((None,) + shape, lambda *_: (layer,) + (0,) * len(shape))
    return pl.BlockSpec(a.shape, lambda *_: (0,) * a.ndim)


def _operands(*args):
    return tuple(a.stacked if isinstance(a, _LayerParam) else a for a in args)


def _cparams(n_axes, flags=None):
    return pltpu.CompilerParams(dimension_semantics=("arbitrary",) * n_axes,
                                vmem_limit_bytes=VMEM_LIMIT, flags=flags)


def _rms(x, w):
    ms = jnp.mean(x * x, axis=-1, keepdims=True)
    return x * lax.rsqrt(ms + NORM_EPS) * w


def _dot(a, b):
    return jnp.dot(a, b, preferred_element_type=F32)


def _dot_nt(a, b, precision=None):
    return lax.dot_general(a, b, _NT, preferred_element_type=F32, precision=precision)


def _silu(x):
    half = 0.5 * x
    return half + half * jnp.tanh(half)


def _rope_table_kernel(pos_ref, inv_ref, cos_ref, sin_ref):
    ang = pos_ref[...].astype(F32) * inv_ref[...]
    lane = lax.broadcasted_iota(jnp.int32, ang.shape, 1)
    first_half = (lane % SWA_HEAD_DIM) < (SWA_HEAD_DIM // 2)
    cos_ref[...] = jnp.cos(ang)
    sin_ref[...] = jnp.where(first_half, -jnp.sin(ang), jnp.sin(ang))


def _rope_tables(positions):
    m = positions.size
    half = SWA_HEAD_DIM // 2
    inv = ROPE_THETA ** (-jnp.arange(half, dtype=F32) / half)
    inv = jnp.tile(inv, LANES // half).reshape(1, LANES)
    t = 1024 if m % 1024 == 0 else ROW_TILE
    return pl.pallas_call(
        _rope_table_kernel,
        out_shape=(jax.ShapeDtypeStruct((m, LANES), F32), jax.ShapeDtypeStruct((m, LANES), F32)),
        grid=(m // t,),
        in_specs=[pl.BlockSpec((t, 1), lambda i: (i, 0)), pl.BlockSpec((1, LANES), lambda i: (0, 0))],
        out_specs=(pl.BlockSpec((t, LANES), lambda i: (i, 0)), pl.BlockSpec((t, LANES), lambda i: (i, 0))),
        compiler_params=_cparams(1),
        name="rope_tables",
    )(positions.reshape(m, 1), inv)


def _rope(t, cos, sin):
    w = t.shape[-1]
    reps = w // LANES
    cos_w = jnp.concatenate([cos] * reps, axis=1)
    sin_w = jnp.concatenate([sin] * reps, axis=1)
    half = SWA_HEAD_DIM // 2
    lane = lax.broadcasted_iota(jnp.int32, t.shape, 1)
    first_half = (lane % SWA_HEAD_DIM) < half
    partner = jnp.where(first_half, pltpu.roll(t, w - half, 1), pltpu.roll(t, half, 1))
    return t * cos_w + partner * sin_w


def _dup_kv_heads(t):
    swapped = pltpu.roll(t, SWA_HEAD_DIM, 1)
    low = lax.broadcasted_iota(jnp.int32, t.shape, 1) < SWA_HEAD_DIM
    return jnp.concatenate([jnp.where(low, t, swapped), jnp.where(low, swapped, t)], axis=1)


def _proj_in_kernel(nkb, x_ref, nw_ref, w_ref, wvt_ref, wdt_ref, wdtt_ref, cos_ref, sin_ref,
                    qa_ref, ka_ref, va_ref, qb_ref, kb_ref, kmean_ref, vt_ref,
                    z_ref, xbc_ref, dt_ref, dtt_ref):
    i = pl.program_id(0)
    h = _rms(x_ref[...], nw_ref[...]).astype(BF16)
    cos = cos_ref[...]
    sin = sin_ref[...]

    def seg(a, b):
        return _dot(h, w_ref[:, a:b])

    qa_ref[...] = _rope(seg(_C_QA, _C_KA), cos, sin).astype(BF16)
    ka_ref[...] = _dup_kv_heads(_rope(seg(_C_KA, _C_VA), cos, sin)).astype(BF16)
    va_ref[...] = _dup_kv_heads(seg(_C_VA, _C_QB)).astype(BF16)
    qb_ref[...] = seg(_C_QB, _C_KB).astype(BF16)
    kb = seg(_C_KB, _C_VB)
    kb_ref[...] = (kb * LOG2E).astype(BF16)
    per_tile = kb.shape[0] // MOBA_BLOCK
    for j in range(per_tile):
        blk_mean = jnp.mean(kb[j * MOBA_BLOCK:(j + 1) * MOBA_BLOCK], axis=0, keepdims=True)
        kmean_ref[0, pl.ds((i * per_tile + j) % nkb, 1), :] = blk_mean
    vt_ref[...] = _dot_nt(wvt_ref[...], h).astype(BF16)
    z_ref[...] = seg(_C_Z, _C_XBC).astype(BF16)
    xbc_ref[...] = seg(_C_XBC, _C_DT).astype(BF16)
    dt_ref[...] = _dot(h, wdt_ref[...])
    dtt_ref[...] = _dot_nt(wdtt_ref[...], h)


def _proj_in(x2d, norm_w, w_in, w_vt, w_dt, w_dtt, cos_t, sin_t, batch):
    m = x2d.shape[0]
    t = MLP_ROW_TILE
    n = m // t
    nkb = m // batch // MOBA_BLOCK
    per_b = n // batch
    row = lambda w: pl.BlockSpec((t, w), lambda i: (i, 0))
    full = _const_spec
    out_shape = (
        jax.ShapeDtypeStruct((m, SWA_Q_W), BF16),
        jax.ShapeDtypeStruct((m, 2 * SWA_KV_W), BF16),
        jax.ShapeDtypeStruct((m, 2 * SWA_KV_W), BF16),
        jax.ShapeDtypeStruct((m, MOBA_W), BF16),
        jax.ShapeDtypeStruct((m, MOBA_W), BF16),
        jax.ShapeDtypeStruct((batch, nkb, MOBA_W), F32),
        jax.ShapeDtypeStruct((MOBA_W, m), BF16),
        jax.ShapeDtypeStruct((m, SSD_D_INNER), BF16),
        jax.ShapeDtypeStruct((m, SSD_XBC_W), BF16),
        jax.ShapeDtypeStruct((m, DT_PAD), F32),
        jax.ShapeDtypeStruct((SSD_HEADS, m), F32),
    )
    out_specs = (
        row(SWA_Q_W), row(2 * SWA_KV_W), row(2 * SWA_KV_W), row(MOBA_W), row(MOBA_W),
        pl.BlockSpec((1, nkb, MOBA_W), lambda i: (i // per_b, 0, 0)),
        pl.BlockSpec((MOBA_W, t), lambda i: (0, i)),
        row(SSD_D_INNER), row(SSD_XBC_W), row(DT_PAD),
        pl.BlockSpec((SSD_HEADS, t), lambda i: (0, i)),
    )
    return pl.pallas_call(
        functools.partial(_proj_in_kernel, nkb),
        out_shape=out_shape,
        grid=(n,),
        in_specs=[row(D_MODEL), full(norm_w), full(w_in), full(w_vt), full(w_dt), full(w_dtt),
                  row(LANES), row(LANES)],
        out_specs=out_specs,
        compiler_params=_cparams(1),
        name="proj_in",
    )(*_operands(x2d, norm_w, w_in, w_vt, w_dt, w_dtt, cos_t, sin_t))


def _swa_kernel(sink_ref, q_ref, kp_ref, kc_ref, vp_ref, vc_ref, o_ref):
    i = pl.program_id(1)
    blk = SWA_BLOCK
    kall = jnp.concatenate([kp_ref[...], kc_ref[...]], axis=0)
    vall = jnp.concatenate([vp_ref[...], vc_ref[...]], axis=0)
    qi = lax.broadcasted_iota(jnp.int32, (blk, 2 * blk), 0)
    si = lax.broadcasted_iota(jnp.int32, (blk, 2 * blk), 1)
    delta = qi + blk - si
    in_window = (delta >= 0) & (delta < blk)
    lane = lax.broadcasted_iota(jnp.int32, (blk, LANES), 1)
    low = lane < SWA_HEAD_DIM
    group = SWA_Q_HEADS // SWA_KV_HEADS
    for sub in range(SWA_BLOCKS_PER_STEP):
        rows = slice(sub * blk, (sub + 1) * blk)
        mask = in_window & ((si >= blk) | (i > 0)) if sub == 0 else in_window
        q = q_ref[rows, :] * (SWA_HEAD_DIM ** -0.5)
        kcat = kall[sub * blk:(sub + 2) * blk]
        vcat = vall[sub * blk:(sub + 2) * blk]
        outs = []
        for hd in range(SWA_Q_HEADS):
            g = hd // group
            kd = kcat[:, g * LANES:(g + 1) * LANES]
            vd = vcat[:, g * LANES:(g + 1) * LANES]
            slab = q[:, (hd // 2) * LANES:(hd // 2 + 1) * LANES]
            qm = jnp.where(low if hd % 2 == 0 else ~low, slab, jnp.zeros_like(slab))
            s = jnp.where(mask, _dot_nt(qm, kd), -jnp.inf)
            sink = sink_ref[hd]
            mx = jnp.maximum(jnp.max(s, axis=-1, keepdims=True), sink)
            p = jnp.exp(s - mx)
            denom = jnp.sum(p, axis=-1, keepdims=True) + jnp.exp(sink - mx)
            outs.append(_dot(p.astype(BF16), vd) / denom)
        for pr in range(SWA_Q_HEADS // 2):
            o_ref[rows, pr * LANES:(pr + 1) * LANES] = jnp.where(low, outs[2 * pr], outs[2 * pr + 1]).astype(BF16)


def _swa(q_a, k_a, v_a, sinks, batch):
    m = q_a.shape[0]
    blk = SWA_BLOCK
    per = SWA_BLOCKS_PER_STEP
    nb = m // batch // blk
    steps = nb // per
    kvw = 2 * SWA_KV_W
    cur = lambda b, i: (b * steps + i, 0)
    prev = lambda b, i: (b * nb + jnp.maximum(per * i - 1, 0), 0)
    return pl.pallas_call(
        _swa_kernel,
        out_shape=jax.ShapeDtypeStruct((m, SWA_Q_W), BF16),
        grid=(batch, steps),
        in_specs=[pl.BlockSpec(memory_space=pltpu.SMEM),
                  pl.BlockSpec((per * blk, SWA_Q_W), cur),
                  pl.BlockSpec((blk, kvw), prev), pl.BlockSpec((per * blk, kvw), cur),
                  pl.BlockSpec((blk, kvw), prev), pl.BlockSpec((per * blk, kvw), cur)],
        out_specs=pl.BlockSpec((per * blk, SWA_Q_W), cur),
        compiler_params=_cparams(2),
        name="swa",
    )(sinks, q_a, k_a, k_a, v_a, v_a)


def _fold8(x, op):
    return functools.reduce(op, [x[r:r + 8] for r in range(0, x.shape[0], 8)])


def _moba_kernel(q_ref, k_ref, vt_ref, kmean_ref, o_ref, *scratch):
    i = pl.program_id(2)
    per_pipe = len(scratch) // MOBA_PAIRS_PER_STEP
    pipes = []
    for ps in range(MOBA_PAIRS_PER_STEP):
        lanes = pl.ds(ps * HEAD_PAIR_W, HEAD_PAIR_W)
        pipes.append(_moba_pipeline(i, q_ref.at[:, lanes], k_ref.at[:, lanes], vt_ref.at[lanes, :],
                                    kmean_ref.at[:, :, lanes], o_ref.at[:, lanes],
                                    *scratch[ps * per_pipe:(ps + 1) * per_pipe]))
    n_groups = (i + MOBA_GROUP - 1) // MOBA_GROUP

    for p in pipes:
        p["setup"]()
    for p in pipes:
        p["scores_own"]()
    mx0 = [None] * len(pipes)
    for r in range(MOBA_GROUP):
        for n, p in enumerate(pipes):
            mx0[n] = p["score_block"](0, 0, r, mx0[n])
            if r == 0:
                p["accumulate_own"]()
    for n, p in enumerate(pipes):
        p["end_scores"](0, mx0[n])

    def stage(score_g, score_buf, acc_g, acc_buf):
        state = [p["begin"](acc_buf) for p in pipes]
        mx = [None] * len(pipes)
        pv = [None] * len(pipes)
        for r in range(MOBA_GROUP):
            for n, p in enumerate(pipes):
                mx[n] = p["score_block"](score_g, score_buf, r, mx[n])
                pv[n] = p["exp_block"](acc_g, acc_buf, r, state[n], pv[n])
        for n, p in enumerate(pipes):
            p["end_scores"](score_buf, mx[n])
            p["finish"](state[n], pv[n])

    def pair(u, carry):
        g = 2 * u
        stage(g + 1, 1, g, 0)
        stage(g + 2, 0, g + 1, 1)
        return carry

    lax.fori_loop(0, n_groups // 2, pair, 0)

    @pl.when(n_groups % 2 == 1)
    def _():
        for p in pipes:
            p["accumulate"](n_groups - 1, 0)

    for p in pipes:
        p["finalize"]()


def _moba_pipeline(i, q_ref, k_ref, vt_ref, kmean_ref, o_ref, qm_ref, sel_ref, ml_ref, acc_ref,
                   so_ref, mxo_ref, po_ref, sa_ref, mxa_ref, pa_ref, sb_ref, mxb_ref, pb_ref):
    blk = MOBA_BLOCK
    hd = MOBA_HEAD_DIM
    grp = MOBA_GROUP
    nkb = kmean_ref.shape[1]
    stage_bufs = ((sa_ref, mxa_ref, pa_ref), (sb_ref, mxb_ref, pb_ref))

    def setup():
        q = q_ref[...] * (hd ** -0.5)
        lane = lax.broadcasted_iota(jnp.int32, q.shape, 1)
        qm_ref[0:blk, :] = jnp.where(lane < hd, q, jnp.zeros_like(q))
        qm_ref[blk:2 * blk, :] = jnp.where(lane >= hd, q, jnp.zeros_like(q))

        blk_id = lax.broadcasted_iota(jnp.int32, (nkb, 2 * blk), 0)
        gate = None
        resid = kmean_ref[0]
        for _ in range(3):
            part = resid.astype(BF16)
            resid = resid - part.astype(F32)
            term = _dot_nt(part, qm_ref[...])
            gate = term if gate is None else gate + term
        gate = jnp.where(blk_id < i, gate, -jnp.inf)
        sel = jnp.zeros((nkb, 2 * blk), F32)
        for _ in range(min(MOBA_TOPK, nkb)):
            best = jnp.max(gate, axis=0, keepdims=True)
            idx = jnp.min(jnp.where(gate == best, blk_id, nkb), axis=0, keepdims=True)
            pick = blk_id == idx
            sel = jnp.where(pick & (blk_id < i), 1.0, sel)
            gate = jnp.where(pick, -jnp.inf, gate)
        sel_ref[0:nkb, :] = sel
        sel_ref[nkb:nkb + grp, :] = jnp.zeros((grp, 2 * blk), F32)

    def score_block(start, r, keep, s_buf, mx):
        s = _dot_nt(k_ref[pl.ds(start + r * blk, blk), :], qm_ref[...])
        s = jnp.where(keep, s, -jnp.inf)
        s_buf[r * blk:(r + 1) * blk, :] = s
        m8 = _fold8(s, jnp.maximum)
        return m8 if mx is None else jnp.maximum(mx, m8)

    def begin(mx_buf, first):
        mx = jnp.max(mx_buf[...], axis=0, keepdims=True)
        if first:
            m_new, alpha = mx, None
        else:
            m_old = ml_ref[0:1, :]
            m_new = jnp.maximum(m_old, mx)
            alpha = jnp.exp2(m_old - m_new)
        ml_ref[0:1, :] = m_new
        return m_new, alpha

    def exp_block(s_buf, p_buf, r, m_new):
        p_buf[r * blk:(r + 1) * blk, :] = jnp.exp2(s_buf[r * blk:(r + 1) * blk, :] - m_new).astype(BF16)

    def pv_block(start, r, p_buf, pv):
        ones = jnp.ones((SUM_ROWS, blk), BF16)
        out = []
        for h in range(2):
            vt_ext = jnp.concatenate([vt_ref[h * hd:(h + 1) * hd, pl.ds(start + r * blk, blk)], ones], axis=0)
            term = _dot(vt_ext, p_buf[r * blk:(r + 1) * blk, h * blk:(h + 1) * blk])
            out.append(term if pv is None else pv[h] + term)
        return out

    def finish(pv, alpha):
        for h in range(2):
            acc_ref[h] = pv[h] if alpha is None else alpha[:, h * blk:(h + 1) * blk] * acc_ref[h] + pv[h]

    def group_start(g):
        return pl.multiple_of(jnp.minimum(g, nkb // grp - 1) * (grp * blk), grp * blk)

    def group_score_block(g, buf, r, mx):
        keep = sel_ref[pl.ds(g * grp + r, 1), :] > 0.0
        return score_block(group_start(g), r, keep, stage_bufs[buf][0], mx)

    def group_end_scores(buf, mx):
        stage_bufs[buf][1][...] = mx

    def group_begin(buf):
        return begin(stage_bufs[buf][1], False)

    def group_exp_block(g, buf, r, state, pv):
        exp_block(stage_bufs[buf][0], stage_bufs[buf][2], r, state[0])
        return pv_block(group_start(g), r, stage_bufs[buf][2], pv)

    def group_finish(state, pv):
        finish(pv, state[1])

    def group_accumulate(g, buf):
        state = group_begin(buf)
        pv = None
        for r in range(grp):
            pv = group_exp_block(g, buf, r, state, pv)
        group_finish(state, pv)

    own = pl.multiple_of(i * blk, blk)

    def scores_own():
        kpos = lax.broadcasted_iota(jnp.int32, (blk, 2 * blk), 0)
        qpos = lax.broadcasted_iota(jnp.int32, (blk, 2 * blk), 1) % blk
        mxo_ref[...] = score_block(own, 0, kpos <= qpos, so_ref, None)

    def accumulate_own():
        m_new, alpha = begin(mxo_ref, True)
        exp_block(so_ref, po_ref, 0, m_new)
        finish(pv_block(own, 0, po_ref, None), alpha)

    def finalize():
        out_t = jnp.concatenate([acc_ref[h, 0:hd, :] / acc_ref[h, hd:hd + 1, :] for h in range(2)], axis=0)
        o_ref[...] = out_t.T.astype(BF16)

    return dict(setup=setup, scores_own=scores_own, accumulate_own=accumulate_own,
                accumulate=group_accumulate, finalize=finalize, begin=group_begin,
                score_block=group_score_block, exp_block=group_exp_block, end_scores=group_end_scores,
                finish=group_finish)


def _moba_stage(n_blocks, blk):
    return [pltpu.VMEM((n_blocks * blk, 2 * blk), F32), pltpu.VMEM((8, 2 * blk), F32),
            pltpu.VMEM((n_blocks * blk, 2 * blk), BF16)]


def _moba(q_b, k_b, vt_b, kmean, batch):
    m = q_b.shape[0]
    s = m // batch
    blk = MOBA_BLOCK
    nkb = s // blk
    assert nkb % MOBA_GROUP == 0
    w = MOBA_PAIRS_PER_STEP * HEAD_PAIR_W
    steps = MOBA_W // w
    pipe_scratch = [pltpu.VMEM((2 * blk, HEAD_PAIR_W), BF16),
                    pltpu.VMEM((nkb + MOBA_GROUP, 2 * blk), F32),
                    pltpu.VMEM((8, 2 * blk), F32),
                    pltpu.VMEM((2, MOBA_HEAD_DIM + SUM_ROWS, blk), F32),
                    ] + _moba_stage(1, blk) + _moba_stage(MOBA_GROUP, blk) + _moba_stage(MOBA_GROUP, blk)
    return pl.pallas_call(
        _moba_kernel,
        out_shape=jax.ShapeDtypeStruct((m, MOBA_W), BF16),
        grid=(batch, steps, nkb),
        in_specs=[pl.BlockSpec((blk, w), lambda b, p, i: (b * nkb + i, p)),
                  pl.BlockSpec((s, w), lambda b, p, i: (b, p)),
                  pl.BlockSpec((w, s), lambda b, p, i: (p, b)),
                  pl.BlockSpec((1, nkb, w), lambda b, p, i: (b, 0, p))],
        out_specs=pl.BlockSpec((blk, w), lambda b, p, i: (b * nkb + i, p)),
        scratch_shapes=pipe_scratch * MOBA_PAIRS_PER_STEP,
        compiler_params=_cparams(3),
        name="moba",
    )(q_b, k_b, vt_b, kmean)


def _split_dot(v, e):
    hi = v.astype(BF16)
    lo = (v - hi.astype(F32)).astype(BF16)
    return _dot(hi, e) + _dot(lo, e)


def _ssd_kernel(z_ref, xbc_ref, dt_ref, dtt_ref, cw_ref, cb_ref, dtb_ref, dtbc_ref, alog_ref, alogc_ref,
                dskip_ref, nw_ref, exp_ref, shift_ref, o_ref, ext_ref, state_ref):
    c = pl.program_id(1)
    L = SSD_CHUNK
    tail = CONV_TAIL

    last = SSD_CHUNKS_PER_STEP * L

    @pl.when(c > 0)
    def _():
        ext_ref[0:tail, :] = ext_ref[last:last + tail, :]

    @pl.when(c == 0)
    def _():
        ext_ref[0:tail, :] = jnp.zeros((tail, SSD_XBC_W), BF16)
        state_ref[...] = jnp.zeros_like(state_ref)

    ext_ref[tail:tail + last, :] = xbc_ref[...]
    for sub in range(SSD_CHUNKS_PER_STEP):
        rows = slice(sub * L, (sub + 1) * L)
        window = ext_ref[sub * L:(sub + 1) * L + tail, :]
        _ssd_chunk(window, xbc_ref[rows, :], z_ref[rows, :], dt_ref[rows, :], dtt_ref[:, rows], cw_ref, cb_ref,
                   dtb_ref, dtbc_ref, alog_ref, alogc_ref, dskip_ref, nw_ref, exp_ref, shift_ref,
                   o_ref.at[rows, :], state_ref)


def _ssd_chunk(window, xbc, z, dt_raw, dtt_raw, cw_ref, cb_ref, dtb_ref, dtbc_ref, alog_ref, alogc_ref,
               dskip_ref, nw_ref, exp_ref, shift_ref, o_ref, state_ref):
    L = SSD_CHUNK
    shifted = _dot(shift_ref[...], window)
    acc = cb_ref[...] + cw_ref[SSD_CONV - 1:SSD_CONV, :] * xbc.astype(F32)
    for j in range(SSD_CONV - 1):
        acc = acc + cw_ref[j:j + 1, :] * shifted[j * L:(j + 1) * L, :]
    xc = _silu(acc)
    xs = xc[:, :SSD_D_INNER]
    bm = xc[:, SSD_D_INNER:SSD_D_INNER + SSD_BC_W]
    cm = xc[:, SSD_D_INNER + SSD_BC_W:]

    dt = jax.nn.softplus(dt_raw + dtb_ref[...])
    dtt = jax.nn.softplus(dtt_raw + dtbc_ref[...])
    a_row = -jnp.exp(alog_ref[...])
    a_col = -jnp.exp(alogc_ref[...])
    r_i = lax.broadcasted_iota(jnp.int32, (L, L), 0)
    c_i = lax.broadcasted_iota(jnp.int32, (L, L), 1)
    tril = r_i >= c_i
    hp = lax.Precision.HIGHEST
    a_cs = jnp.dot(tril.astype(F32), dt * a_row, precision=hp, preferred_element_type=F32)
    a_cst = jnp.dot(dtt * a_col, (r_i <= c_i).astype(F32), precision=hp, preferred_element_type=F32)
    a_last = a_cs[L - 1:L, :]
    expand = exp_ref[...]
    decay_out = _split_dot(jnp.exp(a_cs), expand)
    decay_in = _split_dot(dt * jnp.exp(a_last - a_cs), expand)
    chunk_decay = decay_out[L - 1:L, :]

    lane = lax.broadcasted_iota(jnp.int32, (L, LANES), 1)
    low = lane < SSD_HEAD_DIM
    heads_per_group = SSD_HEADS // SSD_GROUPS
    ys = []
    for g in range(SSD_GROUPS):
        b_g = bm[:, g * SSD_D_STATE:(g + 1) * SSD_D_STATE]
        c_g = cm[:, g * SSD_D_STATE:(g + 1) * SSD_D_STATE]
        cb = _dot_nt(c_g.astype(BF16), b_g.astype(BF16))
        gcols = slice(g * SSD_GROUP_W, (g + 1) * SSD_GROUP_W)
        y_off = _dot(c_g.astype(BF16), state_ref[g].astype(BF16)) * decay_out[:, gcols]
        y_diag = []
        for pr in range(heads_per_group // 2):
            col0 = g * SSD_GROUP_W + pr * LANES
            x_pair = xs[:, col0:col0 + LANES].astype(BF16)
            halves = []
            for e in range(2):
                hidx = g * heads_per_group + 2 * pr + e
                diff = a_cs[:, hidx:hidx + 1] - a_cst[hidx:hidx + 1, :]
                wmat = cb * jnp.exp(jnp.where(tril, diff, -jnp.inf)) * dtt[hidx:hidx + 1, :]
                halves.append(_dot(wmat.astype(BF16), x_pair))
            y_diag.append(jnp.where(low, halves[0], halves[1]))
        ys.append(jnp.concatenate(y_diag, axis=1) + y_off)
        xw = (xs[:, gcols] * decay_in[:, gcols]).astype(BF16)
        state_ref[g] = state_ref[g] * chunk_decay[:, gcols] + _dot(b_g.T.astype(BF16), xw)

    y = jnp.concatenate(ys, axis=1) + xs * dskip_ref[...]
    y = y * _silu(z.astype(F32))
    outs = []
    for g in range(SSD_GROUPS):
        yg = y[:, g * SSD_GROUP_W:(g + 1) * SSD_GROUP_W]
        outs.append(yg * lax.rsqrt(jnp.mean(yg * yg, axis=-1, keepdims=True) + NORM_EPS))
    o_ref[...] = (jnp.concatenate(outs, axis=1) * nw_ref[...]).astype(BF16)


def _ssd(z, xbc, dt, dtt, sp, batch):
    m = z.shape[0]
    L = SSD_CHUNKS_PER_STEP * SSD_CHUNK
    nc = m // batch // L
    row = lambda w: pl.BlockSpec((L, w), lambda b, c: (b * nc + c, 0))
    full = _const_spec
    consts = (sp["conv_w"], sp["conv_b"], sp["dt_bias_row"], sp["dt_bias_col"], sp["a_log_row"],
              sp["a_log_col"], sp["d_skip"], sp["norm_w"], sp["expand"], sp["shift"])
    return pl.pallas_call(
        _ssd_kernel,
        out_shape=jax.ShapeDtypeStruct((m, SSD_D_INNER), BF16),
        grid=(batch, nc),
        in_specs=[row(SSD_D_INNER), row(SSD_XBC_W), row(DT_PAD),
                  pl.BlockSpec((SSD_HEADS, L), lambda b, c: (0, b * nc + c))] + [full(a) for a in consts],
        out_specs=row(SSD_D_INNER),
        scratch_shapes=[pltpu.VMEM((L + CONV_TAIL, SSD_XBC_W), BF16),
                        pltpu.VMEM((SSD_GROUPS, SSD_D_STATE, SSD_GROUP_W), F32)],
        compiler_params=_cparams(2),
        name="ssd",
    )(*_operands(z, xbc, dt, dtt, *consts))


def _mem_kv_kernel(mem_ref, nw_ref, w_ref, k_ref, v_ref):
    hm = _rms(mem_ref[...], nw_ref[...]).astype(BF16)
    k_ref[...] = _dot(hm, w_ref[:, :D_MODEL]).astype(BF16)
    v_ref[...] = _dot(hm, w_ref[:, D_MODEL:]).astype(BF16)


def _mem_kv(mem2d, norm_w, w_ckv):
    m = mem2d.shape[0]
    t = MEM_LEN
    row = pl.BlockSpec((t, D_MODEL), lambda i: (i, 0))
    full = _const_spec
    return pl.pallas_call(
        _mem_kv_kernel,
        out_shape=(jax.ShapeDtypeStruct((m, D_MODEL), BF16), jax.ShapeDtypeStruct((m, D_MODEL), BF16)),
        grid=(m // t,),
        in_specs=[row, full(norm_w), full(w_ckv)],
        out_specs=(row, row),
        compiler_params=_cparams(1),
        name="mem_kv",
    )(*_operands(mem2d, norm_w, w_ckv))


def _merge_kernel(x_ref, ya_ref, yb_ref, yc_ref, nmix_ref, wg_ref, bg_ref, wa_ref, wb_ref, wc_ref, wmix_ref,
                  ncross_ref, wcq_ref, kmem_ref, vmem_ref, wco_ref, o_ref):
    x = x_ref[...]
    h = _rms(x, nmix_ref[...]).astype(BF16)
    merged = None
    for br, (y_ref, w_ref) in enumerate(((ya_ref, wa_ref), (yb_ref, wb_ref), (yc_ref, wc_ref))):
        cols = slice(br * D_MODEL, (br + 1) * D_MODEL)
        gate = jax.nn.sigmoid(_dot(h, wg_ref[:, cols]) + bg_ref[:, cols])
        term = gate * _dot(y_ref[...], w_ref[...])
        merged = term if merged is None else merged + term
    x = x + _dot(merged.astype(BF16), wmix_ref[...])

    hq = _rms(x, ncross_ref[...]).astype(BF16)
    q = _dot(hq, wcq_ref[...]).astype(BF16)
    scale = X_HEAD_DIM ** -0.5
    outs = []
    for hh in range(X_HEADS):
        cols = slice(hh * X_HEAD_DIM, (hh + 1) * X_HEAD_DIM)
        s = _dot_nt(q[:, cols], kmem_ref[:, cols]) * scale
        p = jnp.exp(s - jnp.max(s, axis=-1, keepdims=True))
        denom = jnp.sum(p, axis=-1, keepdims=True)
        outs.append(_dot(p.astype(BF16), vmem_ref[:, cols]) / denom)
    o = jnp.concatenate(outs, axis=1).astype(BF16)
    o_ref[...] = x + _dot(o, wco_ref[...])


def _merge(x2d, y_a, y_b, y_c, kmem, vmem, lp, batch):
    m = x2d.shape[0]
    t = MLP_ROW_TILE
    n = m // t
    per_b = n // batch
    row = lambda w: pl.BlockSpec((t, w), lambda i: (i, 0))
    full = _const_spec
    memspec = pl.BlockSpec((MEM_LEN, D_MODEL), lambda i: (i // per_b, 0))
    return pl.pallas_call(
        _merge_kernel,
        out_shape=jax.ShapeDtypeStruct((m, D_MODEL), F32),
        grid=(n,),
        in_specs=[row(D_MODEL), row(SWA_Q_W), row(MOBA_W), row(SSD_D_INNER),
                  full(lp["norm_mix"]), full(lp["w_gate"]), full(lp["b_gate"]),
                  full(lp["w_br_swa"]), full(lp["w_br_moba"]), full(lp["w_br_ssd"]), full(lp["w_mix_out"]),
                  full(lp["norm_cross"]), full(lp["w_cq"]), memspec, memspec, full(lp["w_co"])],
        out_specs=row(D_MODEL),
        compiler_params=_cparams(1),
        name="merge_cross",
    )(*_operands(x2d, y_a, y_b, y_c, lp["norm_mix"], lp["w_gate"], lp["b_gate"], lp["w_br_swa"], lp["w_br_moba"],
                 lp["w_br_ssd"], lp["w_mix_out"], lp["norm_cross"], lp["w_cq"], kmem, vmem, lp["w_co"]))


def _mlp_kernel(final, x_ref, nw_ref, wup_ref, wdown_ref, fnw_ref, o_ref):
    x = x_ref[...]
    hf = _rms(x, nw_ref[...]).astype(BF16)
    chunk = D_MODEL
    acc = x
    for c in range(D_FF // chunk):
        cols = slice(c * chunk, (c + 1) * chunk)
        u = jnp.maximum(_dot(hf, wup_ref[:, cols]), 0.0)
        acc = acc + _dot((u * u).astype(BF16), wdown_ref[cols, :])
    if final:
        acc = _rms(acc, fnw_ref[...])
    o_ref[...] = acc


def _mlp(x2d, norm_w, w_up, w_down, final_norm_w, final):
    m = x2d.shape[0]
    t = MLP_ROW_TILE
    row = pl.BlockSpec((t, D_MODEL), lambda i: (i, 0))
    full = _const_spec
    return pl.pallas_call(
        functools.partial(_mlp_kernel, final),
        out_shape=jax.ShapeDtypeStruct((m, D_MODEL), F32),
        grid=(m // t,),
        in_specs=[row, full(norm_w), full(w_up), full(w_down), full(final_norm_w)],
        out_specs=row,
        compiler_params=_cparams(1),
        name="mlp_final" if final else "mlp",
    )(*_operands(x2d, norm_w, w_up, w_down, final_norm_w))


_MATMUL_WEIGHTS = ("w_in", "w_gate", "w_br_swa", "w_br_moba", "w_br_ssd", "w_mix_out", "w_cq", "w_ckv", "w_co",
                   "w_up", "w_down")
_ROW_PARAMS = ("norm_mix", "b_gate", "norm_cross", "norm_mem", "norm_mlp", "conv_b", "ssd_norm")


def _prep_params(p):
    depth = p["w_in"].shape[0]
    out = {k: p[k].astype(BF16) for k in _MATMUL_WEIGHTS}
    out.update({k: p[k].astype(F32).reshape(depth, 1, -1) for k in _ROW_PARAMS})
    out["w_dt"] = jnp.pad(out["w_in"][:, :, _C_DT:], ((0, 0), (0, 0), (0, DT_PAD - SSD_HEADS)))
    out["w_vt"] = jnp.swapaxes(out["w_in"][:, :, _C_VB:_C_Z], 1, 2)
    out["w_dtt"] = jnp.swapaxes(out["w_in"][:, :, _C_DT:], 1, 2)
    pad_row = lambda v: jnp.pad(v.astype(F32), ((0, 0), (0, DT_PAD - SSD_HEADS))).reshape(depth, 1, DT_PAD)
    col = lambda v: v.astype(F32).reshape(depth, SSD_HEADS, 1)
    out.update(conv_w=p["conv_w"].astype(F32), dt_bias_row=pad_row(p["dt_bias"]), dt_bias_col=col(p["dt_bias"]),
               a_log_row=pad_row(p["a_log"]), a_log_col=col(p["a_log"]),
               d_skip=jnp.repeat(p["d_skip"].astype(F32), SSD_HEAD_DIM, axis=1).reshape(depth, 1, SSD_D_INNER),
               swa_sinks=p["swa_sinks"].astype(F32))
    return out


def _ssd_constants():
    head_of_col = jnp.arange(SSD_D_INNER) // SSD_HEAD_DIM
    expand = (jnp.arange(DT_PAD)[:, None] == head_of_col[None, :]).astype(BF16)
    t_idx = jnp.arange(SSD_CHUNK)[None, :, None]
    j_idx = jnp.arange(SSD_CONV - 1)[:, None, None]
    col = jnp.arange(SSD_CHUNK + CONV_TAIL)[None, None, :]
    shift = (col == CONV_TAIL + t_idx - (SSD_CONV - 1) + j_idx).astype(BF16)
    return expand, shift.reshape((SSD_CONV - 1) * SSD_CHUNK, SSD_CHUNK + CONV_TAIL)


def _layer_params(l, sp, expand, shift):
    lp = {k: _LayerParam(v, l) for k, v in sp.items() if k != "swa_sinks"}
    lp["swa_sinks"] = sp["swa_sinks"][l]
    lp["ssd"] = {"conv_w": lp["conv_w"], "conv_b": lp["conv_b"], "dt_bias_row": lp["dt_bias_row"],
                 "dt_bias_col": lp["dt_bias_col"], "a_log_row": lp["a_log_row"], "a_log_col": lp["a_log_col"],
                 "d_skip": lp["d_skip"], "norm_w": lp["ssd_norm"], "expand": expand, "shift": shift}
    return lp


def kernel(x, mem, positions, norm_mix, w_in, w_gate, b_gate, swa_sinks, conv_w, conv_b, dt_bias, a_log, d_skip,
           ssd_norm, w_br_swa, w_br_moba, w_br_ssd, w_mix_out, norm_cross, norm_mem, w_cq, w_ckv, w_co, norm_mlp,
           w_up, w_down, final_norm):
    batch, seq, d = x.shape
    depth = w_in.shape[0]
    assert d == D_MODEL and seq % MOBA_BLOCK == 0 and mem.shape[1] == MEM_LEN
    params = dict(norm_mix=norm_mix, w_in=w_in, w_gate=w_gate, b_gate=b_gate, swa_sinks=swa_sinks, conv_w=conv_w,
                  conv_b=conv_b, dt_bias=dt_bias, a_log=a_log, d_skip=d_skip, ssd_norm=ssd_norm, w_br_swa=w_br_swa,
                  w_br_moba=w_br_moba, w_br_ssd=w_br_ssd, w_mix_out=w_mix_out, norm_cross=norm_cross,
                  norm_mem=norm_mem, w_cq=w_cq, w_ckv=w_ckv, w_co=w_co, norm_mlp=norm_mlp, w_up=w_up, w_down=w_down)
    x2d = x.reshape(batch * seq, d)
    mem2d = mem.reshape(batch * MEM_LEN, d)
    fnw = final_norm.reshape(1, d).astype(F32)
    cos_t, sin_t = _rope_tables(positions)
    stacked = _prep_params(params)
    expand, shift = _ssd_constants()
    for l in range(depth):
        lp = _layer_params(l, stacked, expand, shift)
        (q_a, k_a, v_a, q_b, k_b, kmean, vt_b, z, xbc, dt, dtt) = _proj_in(
            x2d, lp["norm_mix"], lp["w_in"], lp["w_vt"], lp["w_dt"], lp["w_dtt"], cos_t, sin_t, batch)
        y_a = _swa(q_a, k_a, v_a, lp["swa_sinks"], batch)
        y_b = _moba(q_b, k_b, vt_b, kmean, batch)
        y_c = _ssd(z, xbc, dt, dtt, lp["ssd"], batch)
        kmem, vmem = _mem_kv(mem2d, lp["norm_mem"], lp["w_ckv"])
        x2d = _merge(x2d, y_a, y_b, y_c, kmem, vmem, lp, batch)
        x2d = _mlp(x2d, lp["norm_mlp"], lp["w_up"], lp["w_down"], fnw, final=(l == depth - 1))
    return x2d.reshape(batch, seq, d)
```

```python
import functools
import math
from typing import NamedTuple

import jax
import jax.numpy as jnp
from jax import lax
from jax.experimental import pallas as pl
from jax.experimental.pallas import tpu as pltpu

F32 = jnp.float32
BF16 = jnp.bfloat16

D_MODEL = 1024
MEM_LEN = 256
NORM_EPS = 1e-6
SWA_Q_HEADS = 8
SWA_KV_HEADS = 2
SWA_HEAD_DIM = 64
SWA_BLOCK = 128
SWA_BLOCKS_PER_STEP = 2
ROPE_THETA = 150000.0
MOBA_HEADS = 8
MOBA_HEAD_DIM = 64
MOBA_BLOCK = 256
MOBA_TOPK = 3
MOBA_GROUP = 4
MOBA_PAIRS_PER_STEP = 2
SUM_ROWS = 16
MOBA_VT_ROWS = MOBA_HEADS * (MOBA_HEAD_DIM + SUM_ROWS)
SSD_D_INNER = D_MODEL
SSD_HEAD_DIM = 64
SSD_HEADS = SSD_D_INNER // SSD_HEAD_DIM
SSD_GROUPS = 2
SSD_D_STATE = 128
SSD_CONV = 4
SSD_CHUNK = 128
SSD_CHUNKS_PER_STEP = 2
X_HEADS = 4
X_HEAD_DIM = D_MODEL // X_HEADS
D_FF = 4 * D_MODEL
N_BRANCH = 3

SWA_Q_W = SWA_Q_HEADS * SWA_HEAD_DIM
SWA_KV_W = SWA_KV_HEADS * SWA_HEAD_DIM
MOBA_W = MOBA_HEADS * MOBA_HEAD_DIM
SSD_BC_W = SSD_GROUPS * SSD_D_STATE
SSD_XBC_W = SSD_D_INNER + 2 * SSD_BC_W
SSD_GROUP_W = SSD_D_INNER // SSD_GROUPS

LANES = 128
HEAD_PAIR_W = 2 * MOBA_HEAD_DIM
DT_PAD = LANES
CONV_TAIL = 16
ROW_TILE = 256
MLP_ROW_TILE = 512
VMEM_LIMIT = 56 * 1024 * 1024

_C_QA = 0
_C_KA = _C_QA + SWA_Q_W
_C_VA = _C_KA + SWA_KV_W
_C_QB = _C_VA + SWA_KV_W
_C_KB = _C_QB + MOBA_W
_C_VB = _C_KB + MOBA_W
_C_Z = _C_VB + MOBA_W
_C_XBC = _C_Z + SSD_D_INNER
_C_DT = _C_XBC + SSD_XBC_W
D_IN_PROJ = _C_DT + SSD_HEADS

LOG2E = math.log2(math.e)
_NT = (((1,), (1,)), ((), ()))


class _LayerParam(NamedTuple):
    stacked: jax.Array
    layer: int


def _const_spec(a):
    if isinstance(a, _LayerParam):
        shape = a.stacked.shape[1:]
        layer = a.layer
        return pl.BlockSpec((None,) + shape, lambda *_: (layer,) + (0,) * len(shape))
    return pl.BlockSpec(a.shape, lambda *_: (0,) * a.ndim)


def _operands(*args):
    return tuple(a.stacked if isinstance(a, _LayerParam) else a for a in args)


def _cparams(n_axes, flags=None):
    return pltpu.CompilerParams(dimension_semantics=("arbitrary",) * n_axes,
                                vmem_limit_bytes=VMEM_LIMIT, flags=flags)


def _rms(x, w):
    ms = jnp.mean(x * x, axis=-1, keepdims=True)
    return x * lax.rsqrt(ms + NORM_EPS) * w


def _dot(a, b):
    return jnp.dot(a, b, preferred_element_type=F32)


def _dot_nt(a, b, precision=None):
    return lax.dot_general(a, b, _NT, preferred_element_type=F32, precision=precision)


def _silu(x):
    half = 0.5 * x
    return half + half * jnp.tanh(half)


def _rope_table_kernel(pos_ref, inv_ref, cos_ref, sin_ref):
    ang = pos_ref[...].astype(F32) * inv_ref[...]
    lane = lax.broadcasted_iota(jnp.int32, ang.shape, 1)
    first_half = (lane % SWA_HEAD_DIM) < (SWA_HEAD_DIM // 2)
    cos_ref[...] = jnp.cos(ang)
    sin_ref[...] = jnp.where(first_half, -jnp.sin(ang), jnp.sin(ang))


def _rope_tables(positions):
    m = positions.size
    half = SWA_HEAD_DIM // 2
    inv = ROPE_THETA ** (-jnp.arange(half, dtype=F32) / half)
    inv = jnp.tile(inv, LANES // half).reshape(1, LANES)
    t = 1024 if m % 1024 == 0 else ROW_TILE
    return pl.pallas_call(
        _rope_table_kernel,
        out_shape=(jax.ShapeDtypeStruct((m, LANES), F32), jax.ShapeDtypeStruct((m, LANES), F32)),
        grid=(m // t,),
        in_specs=[pl.BlockSpec((t, 1), lambda i: (i, 0)), pl.BlockSpec((1, LANES), lambda i: (0, 0))],
        out_specs=(pl.BlockSpec((t, LANES), lambda i: (i, 0)), pl.BlockSpec((t, LANES), lambda i: (i, 0))),
        compiler_params=_cparams(1),
        name="rope_tables",
    )(positions.reshape(m, 1), inv)


def _rope(t, cos, sin):
    w = t.shape[-1]
    reps = w // LANES
    cos_w = jnp.concatenate([cos] * reps, axis=1)
    sin_w = jnp.concatenate([sin] * reps, axis=1)
    half = SWA_HEAD_DIM // 2
    lane = lax.broadcasted_iota(jnp.int32, t.shape, 1)
    first_half = (lane % SWA_HEAD_DIM) < half
    partner = jnp.where(first_half, pltpu.roll(t, w - half, 1), pltpu.roll(t, half, 1))
    return t * cos_w + partner * sin_w


def _dup_kv_heads(t):
    swapped = pltpu.roll(t, SWA_HEAD_DIM, 1)
    low = lax.broadcasted_iota(jnp.int32, t.shape, 1) < SWA_HEAD_DIM
    return jnp.concatenate([jnp.where(low, t, swapped), jnp.where(low, swapped, t)], axis=1)


def _proj_in_kernel(nkb, x_ref, nw_ref, w_ref, wvt_ref, wdt_ref, wdtt_ref, cos_ref, sin_ref,
                    qa_ref, ka_ref, va_ref, qb_ref, kb_ref, kmean_ref, vt_ref,
                    z_ref, xbc_ref, dt_ref, dtt_ref):
    i = pl.program_id(0)
    h = _rms(x_ref[...], nw_ref[...]).astype(BF16)
    cos = cos_ref[...]
    sin = sin_ref[...]

    def seg(a, b):
        return _dot(h, w_ref[:, a:b])

    qa_ref[...] = _rope(seg(_C_QA, _C_KA), cos, sin).astype(BF16)
    ka_ref[...] = _dup_kv_heads(_rope(seg(_C_KA, _C_VA), cos, sin)).astype(BF16)
    va_ref[...] = _dup_kv_heads(seg(_C_VA, _C_QB)).astype(BF16)
    qb_ref[...] = seg(_C_QB, _C_KB).astype(BF16)
    kb = seg(_C_KB, _C_VB)
    kb_ref[...] = (kb * LOG2E).astype(BF16)
    per_tile = kb.shape[0] // MOBA_BLOCK
    for j in range(per_tile):
        blk_mean = jnp.mean(kb[j * MOBA_BLOCK:(j + 1) * MOBA_BLOCK], axis=0, keepdims=True)
        kmean_ref[0, pl.ds((i * per_tile + j) % nkb, 1), :] = blk_mean
    vt = _dot_nt(wvt_ref[...], h).astype(BF16)
    ext = MOBA_HEAD_DIM + SUM_ROWS
    for hh in range(MOBA_HEADS):
        vt_ref[hh * ext:hh * ext + MOBA_HEAD_DIM, :] = vt[hh * MOBA_HEAD_DIM:(hh + 1) * MOBA_HEAD_DIM]
        vt_ref[hh * ext + MOBA_HEAD_DIM:(hh + 1) * ext, :] = jnp.ones((SUM_ROWS, vt.shape[1]), BF16)
    z_ref[...] = seg(_C_Z, _C_XBC).astype(BF16)
    xbc_ref[...] = seg(_C_XBC, _C_DT).astype(BF16)
    dt_ref[...] = _dot(h, wdt_ref[...])
    dtt_ref[...] = _dot_nt(wdtt_ref[...], h)


def _proj_in(x2d, norm_w, w_in, w_vt, w_dt, w_dtt, cos_t, sin_t, batch):
    m = x2d.shape[0]
    t = MLP_ROW_TILE
    n = m // t
    nkb = m // batch // MOBA_BLOCK
    per_b = n // batch
    row = lambda w: pl.BlockSpec((t, w), lambda i: (i, 0))
    full = _const_spec
    out_shape = (
        jax.ShapeDtypeStruct((m, SWA_Q_W), BF16),
        jax.ShapeDtypeStruct((m, 2 * SWA_KV_W), BF16),
        jax.ShapeDtypeStruct((m, 2 * SWA_KV_W), BF16),
        jax.ShapeDtypeStruct((m, MOBA_W), BF16),
        jax.ShapeDtypeStruct((m, MOBA_W), BF16),
        jax.ShapeDtypeStruct((batch, nkb, MOBA_W), F32),
        jax.ShapeDtypeStruct((MOBA_VT_ROWS, m), BF16),
        jax.ShapeDtypeStruct((m, SSD_D_INNER), BF16),
        jax.ShapeDtypeStruct((m, SSD_XBC_W), BF16),
        jax.ShapeDtypeStruct((m, DT_PAD), F32),
        jax.ShapeDtypeStruct((SSD_HEADS, m), F32),
    )
    out_specs = (
        row(SWA_Q_W), row(2 * SWA_KV_W), row(2 * SWA_KV_W), row(MOBA_W), row(MOBA_W),
        pl.BlockSpec((1, nkb, MOBA_W), lambda i: (i // per_b, 0, 0)),
        pl.BlockSpec((MOBA_VT_ROWS, t), lambda i: (0, i)),
        row(SSD_D_INNER), row(SSD_XBC_W), row(DT_PAD),
        pl.BlockSpec((SSD_HEADS, t), lambda i: (0, i)),
    )
    return pl.pallas_call(
        functools.partial(_proj_in_kernel, nkb),
        out_shape=out_shape,
        grid=(n,),
        in_specs=[row(D_MODEL), full(norm_w), full(w_in), full(w_vt), full(w_dt), full(w_dtt),
                  row(LANES), row(LANES)],
        out_specs=out_specs,
        compiler_params=_cparams(1),
        name="proj_in",
    )(*_operands(x2d, norm_w, w_in, w_vt, w_dt, w_dtt, cos_t, sin_t))


def _swa_kernel(sink_ref, q_ref, kp_ref, kc_ref, vp_ref, vc_ref, o_ref):
    i = pl.program_id(1)
    blk = SWA_BLOCK
    kall = jnp.concatenate([kp_ref[...], kc_ref[...]], axis=0)
    vall = jnp.concatenate([vp_ref[...], vc_ref[...]], axis=0)
    qi = lax.broadcasted_iota(jnp.int32, (blk, 2 * blk), 0)
    si = lax.broadcasted_iota(jnp.int32, (blk, 2 * blk), 1)
    delta = qi + blk - si
    in_window = (delta >= 0) & (delta < blk)
    lane = lax.broadcasted_iota(jnp.int32, (blk, LANES), 1)
    low = lane < SWA_HEAD_DIM
    group = SWA_Q_HEADS // SWA_KV_HEADS
    for sub in range(SWA_BLOCKS_PER_STEP):
        rows = slice(sub * blk, (sub + 1) * blk)
        mask = in_window & ((si >= blk) | (i > 0)) if sub == 0 else in_window
        q = q_ref[rows, :] * (SWA_HEAD_DIM ** -0.5)
        kcat = kall[sub * blk:(sub + 2) * blk]
        vcat = vall[sub * blk:(sub + 2) * blk]
        outs = []
        for hd in range(SWA_Q_HEADS):
            g = hd // group
            kd = kcat[:, g * LANES:(g + 1) * LANES]
            vd = vcat[:, g * LANES:(g + 1) * LANES]
            slab = q[:, (hd // 2) * LANES:(hd // 2 + 1) * LANES]
            qm = jnp.where(low if hd % 2 == 0 else ~low, slab, jnp.zeros_like(slab))
            s = jnp.where(mask, _dot_nt(qm, kd), -jnp.inf)
            sink = sink_ref[hd]
            mx = jnp.maximum(jnp.max(s, axis=-1, keepdims=True), sink)
            p = jnp.exp(s - mx)
            denom = jnp.sum(p, axis=-1, keepdims=True) + jnp.exp(sink - mx)
            outs.append(_dot(p.astype(BF16), vd) / denom)
        for pr in range(SWA_Q_HEADS // 2):
            o_ref[rows, pr * LANES:(pr + 1) * LANES] = jnp.where(low, outs[2 * pr], outs[2 * pr + 1]).astype(BF16)


def _swa(q_a, k_a, v_a, sinks, batch):
    m = q_a.shape[0]
    blk = SWA_BLOCK
    per = SWA_BLOCKS_PER_STEP
    nb = m // batch // blk
    steps = nb // per
    kvw = 2 * SWA_KV_W
    cur = lambda b, i: (b * steps + i, 0)
    prev = lambda b, i: (b * nb + jnp.maximum(per * i - 1, 0), 0)
    return pl.pallas_call(
        _swa_kernel,
        out_shape=jax.ShapeDtypeStruct((m, SWA_Q_W), BF16),
        grid=(batch, steps),
        in_specs=[pl.BlockSpec(memory_space=pltpu.SMEM),
                  pl.BlockSpec((per * blk, SWA_Q_W), cur),
                  pl.BlockSpec((blk, kvw), prev), pl.BlockSpec((per * blk, kvw), cur),
                  pl.BlockSpec((blk, kvw), prev), pl.BlockSpec((per * blk, kvw), cur)],
        out_specs=pl.BlockSpec((per * blk, SWA_Q_W), cur),
        compiler_params=_cparams(2),
        name="swa",
    )(sinks, q_a, k_a, k_a, v_a, v_a)


def _fold8(x, op):
    return functools.reduce(op, [x[r:r + 8] for r in range(0, x.shape[0], 8)])


def _moba_kernel(q_ref, k_ref, vt_ref, kmean_ref, o_ref, *scratch):
    i = pl.program_id(2)
    per_pipe = len(scratch) // MOBA_PAIRS_PER_STEP
    pipes = []
    for ps in range(MOBA_PAIRS_PER_STEP):
        lanes = pl.ds(ps * HEAD_PAIR_W, HEAD_PAIR_W)
        vt_rows = pl.ds(ps * 2 * (MOBA_HEAD_DIM + SUM_ROWS), 2 * (MOBA_HEAD_DIM + SUM_ROWS))
        pipes.append(_moba_pipeline(i, q_ref.at[:, lanes], k_ref.at[:, lanes], vt_ref.at[vt_rows, :],
                                    kmean_ref.at[:, :, lanes], o_ref.at[:, lanes],
                                    *scratch[ps * per_pipe:(ps + 1) * per_pipe]))
    n_groups = (i + MOBA_GROUP - 1) // MOBA_GROUP

    for p in pipes:
        p["setup"]()
    for p in pipes:
        p["scores_own"]()
    mx0 = [None] * len(pipes)
    for r in range(MOBA_GROUP):
        for n, p in enumerate(pipes):
            mx0[n] = p["score_block"](0, 0, r, mx0[n])
            if r == 0:
                p["accumulate_own"]()
    for n, p in enumerate(pipes):
        p["end_scores"](0, mx0[n])

    def stage(score_g, score_buf, acc_g, acc_buf):
        state = [p["begin"](acc_buf) for p in pipes]
        mx = [None] * len(pipes)
        for r in range(MOBA_GROUP):
            for n, p in enumerate(pipes):
                mx[n] = p["score_block"](score_g, score_buf, r, mx[n])
                p["exp_block"](acc_g, acc_buf, r, state[n])
        for n, p in enumerate(pipes):
            p["end_scores"](score_buf, mx[n])

    def pair(u, carry):
        g = 2 * u
        stage(g + 1, 1, g, 0)
        stage(g + 2, 0, g + 1, 1)
        return carry

    lax.fori_loop(0, n_groups // 2, pair, 0)

    @pl.when(n_groups % 2 == 1)
    def _():
        for p in pipes:
            p["accumulate"](n_groups - 1, 0)

    for p in pipes:
        p["finalize"]()


def _moba_pipeline(i, q_ref, k_ref, vt_ref, kmean_ref, o_ref, qm_ref, sel_ref, ml_ref, acc_ref,
                   so_ref, mxo_ref, po_ref, sa_ref, mxa_ref, pa_ref, sb_ref, mxb_ref, pb_ref):
    blk = MOBA_BLOCK
    hd = MOBA_HEAD_DIM
    grp = MOBA_GROUP
    nkb = kmean_ref.shape[1]
    stage_bufs = ((sa_ref, mxa_ref, pa_ref), (sb_ref, mxb_ref, pb_ref))

    def setup():
        qt = (q_ref[...].astype(F32) * (hd ** -0.5)).T
        row = lax.broadcasted_iota(jnp.int32, qt.shape, 0)
        qm_ref[:, 0:blk] = jnp.where(row < hd, qt, 0.0).astype(BF16)
        qm_ref[:, blk:2 * blk] = jnp.where(row >= hd, qt, 0.0).astype(BF16)

        blk_id = lax.broadcasted_iota(jnp.int32, (nkb, 2 * blk), 0)
        gate = None
        resid = kmean_ref[0]
        for _ in range(3):
            part = resid.astype(BF16)
            resid = resid - part.astype(F32)
            term = _dot(part, qm_ref[...])
            gate = term if gate is None else gate + term
        gate = jnp.where(blk_id < i, gate, -jnp.inf)
        sel = jnp.zeros((nkb, 2 * blk), F32)
        for _ in range(min(MOBA_TOPK, nkb)):
            best = jnp.max(gate, axis=0, keepdims=True)
            idx = jnp.min(jnp.where(gate == best, blk_id, nkb), axis=0, keepdims=True)
            pick = blk_id == idx
            sel = jnp.where(pick & (blk_id < i), 1.0, sel)
            gate = jnp.where(pick, -jnp.inf, gate)
        sel_ref[0:nkb, :] = sel
        sel_ref[nkb:nkb + grp, :] = jnp.zeros((grp, 2 * blk), F32)

    def score_block(start, r, keep, s_buf, mx):
        s = _dot(k_ref[pl.ds(start + r * blk, blk), :], qm_ref[...])
        s = jnp.where(keep, s, -jnp.inf)
        s_buf[r * blk:(r + 1) * blk, :] = s
        m8 = _fold8(s, jnp.maximum)
        return m8 if mx is None else jnp.maximum(mx, m8)

    def begin(mx_buf, first):
        mx = jnp.max(mx_buf[...], axis=0, keepdims=True)
        if first:
            m_new, alpha = mx, None
        else:
            m_old = ml_ref[0:1, :]
            m_new = jnp.maximum(m_old, mx)
            alpha = jnp.exp2(m_old - m_new)
            for h in range(2):
                acc_ref[h] = alpha[:, h * blk:(h + 1) * blk] * acc_ref[h]
        ml_ref[0:1, :] = m_new
        return m_new, alpha

    def exp_block(s_buf, p_buf, r, m_new):
        p_buf[r * blk:(r + 1) * blk, :] = jnp.exp2(s_buf[r * blk:(r + 1) * blk, :] - m_new).astype(BF16)

    def pv_block(start, r, p_buf, assign):
        ext = hd + SUM_ROWS
        for h in range(2):
            vt_ext = vt_ref[h * ext:(h + 1) * ext, pl.ds(start + r * blk, blk)]
            term = _dot(vt_ext, p_buf[r * blk:(r + 1) * blk, h * blk:(h + 1) * blk])
            acc_ref[h] = term if assign else acc_ref[h] + term

    def group_start(g):
        return pl.multiple_of(jnp.minimum(g, nkb // grp - 1) * (grp * blk), grp * blk)

    def group_score_block(g, buf, r, mx):
        keep = sel_ref[pl.ds(g * grp + r, 1), :] > 0.0
        return score_block(group_start(g), r, keep, stage_bufs[buf][0], mx)

    def group_end_scores(buf, mx):
        stage_bufs[buf][1][...] = mx

    def group_begin(buf):
        return begin(stage_bufs[buf][1], False)

    def group_exp_block(g, buf, r, state):
        exp_block(stage_bufs[buf][0], stage_bufs[buf][2], r, state[0])
        pv_block(group_start(g), r, stage_bufs[buf][2], False)

    def group_accumulate(g, buf):
        state = group_begin(buf)
        for r in range(grp):
            group_exp_block(g, buf, r, state)

    own = pl.multiple_of(i * blk, blk)

    def scores_own():
        kpos = lax.broadcasted_iota(jnp.int32, (blk, 2 * blk), 0)
        qpos = lax.broadcasted_iota(jnp.int32, (blk, 2 * blk), 1) % blk
        mxo_ref[...] = score_block(own, 0, kpos <= qpos, so_ref, None)

    def accumulate_own():
        m_new, _ = begin(mxo_ref, True)
        exp_block(so_ref, po_ref, 0, m_new)
        pv_block(own, 0, po_ref, True)

    def finalize():
        out_t = jnp.concatenate([acc_ref[h, 0:hd, :] / acc_ref[h, hd:hd + 1, :] for h in range(2)], axis=0)
        o_ref[...] = out_t.T.astype(BF16)

    return dict(setup=setup, scores_own=scores_own, accumulate_own=accumulate_own,
                accumulate=group_accumulate, finalize=finalize, begin=group_begin,
                score_block=group_score_block, exp_block=group_exp_block, end_scores=group_end_scores)


def _moba_stage(n_blocks, blk):
    return [pltpu.VMEM((n_blocks * blk, 2 * blk), F32), pltpu.VMEM((8, 2 * blk), F32),
            pltpu.VMEM((n_blocks * blk, 2 * blk), BF16)]


def _moba(q_b, k_b, vt_b, kmean, batch):
    m = q_b.shape[0]
    s = m // batch
    blk = MOBA_BLOCK
    nkb = s // blk
    assert nkb % MOBA_GROUP == 0
    w = MOBA_PAIRS_PER_STEP * HEAD_PAIR_W
    steps = MOBA_W // w
    pipe_scratch = [pltpu.VMEM((HEAD_PAIR_W, 2 * blk), BF16),
                    pltpu.VMEM((nkb + MOBA_GROUP, 2 * blk), F32),
                    pltpu.VMEM((8, 2 * blk), F32),
                    pltpu.VMEM((2, MOBA_HEAD_DIM + SUM_ROWS, blk), F32),
                    ] + _moba_stage(1, blk) + _moba_stage(MOBA_GROUP, blk) + _moba_stage(MOBA_GROUP, blk)
    return pl.pallas_call(
        _moba_kernel,
        out_shape=jax.ShapeDtypeStruct((m, MOBA_W), BF16),
        grid=(batch, steps, nkb),
        in_specs=[pl.BlockSpec((blk, w), lambda b, p, i: (b * nkb + i, p)),
                  pl.BlockSpec((s, w), lambda b, p, i: (b, p)),
                  pl.BlockSpec((MOBA_VT_ROWS // steps, s), lambda b, p, i: (p, b)),
                  pl.BlockSpec((1, nkb, w), lambda b, p, i: (b, 0, p))],
        out_specs=pl.BlockSpec((blk, w), lambda b, p, i: (b * nkb + i, p)),
        scratch_shapes=pipe_scratch * MOBA_PAIRS_PER_STEP,
        compiler_params=_cparams(3),
        name="moba",
    )(q_b, k_b, vt_b, kmean)


def _split_dot(v, e):
    hi = v.astype(BF16)
    lo = (v - hi.astype(F32)).astype(BF16)
    return _dot(hi, e) + _dot(lo, e)


def _ssd_kernel(z_ref, xbc_ref, dt_ref, dtt_ref, cw_ref, cb_ref, dtb_ref, dtbc_ref, alog_ref, alogc_ref,
                dskip_ref, nw_ref, exp_ref, shift_ref, o_ref, ext_ref, state_ref):
    c = pl.program_id(1)
    L = SSD_CHUNK
    tail = CONV_TAIL

    last = SSD_CHUNKS_PER_STEP * L

    @pl.when(c > 0)
    def _():
        ext_ref[0:tail, :] = ext_ref[last:last + tail, :]

    @pl.when(c == 0)
    def _():
        ext_ref[0:tail, :] = jnp.zeros((tail, SSD_XBC_W), BF16)
        state_ref[...] = jnp.zeros_like(state_ref)

    ext_ref[tail:tail + last, :] = xbc_ref[...]
    for sub in range(SSD_CHUNKS_PER_STEP):
        rows = slice(sub * L, (sub + 1) * L)
        window = ext_ref[sub * L:(sub + 1) * L + tail, :]
        _ssd_chunk(window, xbc_ref[rows, :], z_ref[rows, :], dt_ref[rows, :], dtt_ref[:, rows], cw_ref, cb_ref,
                   dtb_ref, dtbc_ref, alog_ref, alogc_ref, dskip_ref, nw_ref, exp_ref, shift_ref,
                   o_ref.at[rows, :], state_ref)


def _ssd_chunk(window, xbc, z, dt_raw, dtt_raw, cw_ref, cb_ref, dtb_ref, dtbc_ref, alog_ref, alogc_ref,
               dskip_ref, nw_ref, exp_ref, shift_ref, o_ref, state_ref):
    L = SSD_CHUNK
    shifted = _dot(shift_ref[...], window)
    acc = cb_ref[...] + cw_ref[SSD_CONV - 1:SSD_CONV, :] * xbc.astype(F32)
    for j in range(SSD_CONV - 1):
        acc = acc + cw_ref[j:j + 1, :] * shifted[j * L:(j + 1) * L, :]
    xc = _silu(acc)
    xs = xc[:, :SSD_D_INNER]
    bm = xc[:, SSD_D_INNER:SSD_D_INNER + SSD_BC_W]
    cm = xc[:, SSD_D_INNER + SSD_BC_W:]

    dt = jax.nn.softplus(dt_raw + dtb_ref[...])
    dtt = jax.nn.softplus(dtt_raw + dtbc_ref[...])
    a_row = -jnp.exp(alog_ref[...])
    a_col = -jnp.exp(alogc_ref[...])
    r_i = lax.broadcasted_iota(jnp.int32, (L, L), 0)
    c_i = lax.broadcasted_iota(jnp.int32, (L, L), 1)
    tril = r_i >= c_i
    hp = lax.Precision.HIGHEST
    a_cs = jnp.dot(tril.astype(F32), dt * a_row, precision=hp, preferred_element_type=F32)
    a_cst = jnp.dot(dtt * a_col, (r_i <= c_i).astype(F32), precision=hp, preferred_element_type=F32)
    a_last = a_cs[L - 1:L, :]
    expand = exp_ref[...]
    decay_out = _split_dot(jnp.exp(a_cs), expand)
    decay_in = _split_dot(dt * jnp.exp(a_last - a_cs), expand)
    chunk_decay = decay_out[L - 1:L, :]

    lane = lax.broadcasted_iota(jnp.int32, (L, LANES), 1)
    low = lane < SSD_HEAD_DIM
    heads_per_group = SSD_HEADS // SSD_GROUPS
    ys = []
    for g in range(SSD_GROUPS):
        b_g = bm[:, g * SSD_D_STATE:(g + 1) * SSD_D_STATE]
        c_g = cm[:, g * SSD_D_STATE:(g + 1) * SSD_D_STATE]
        cb = _dot_nt(c_g.astype(BF16), b_g.astype(BF16))
        gcols = slice(g * SSD_GROUP_W, (g + 1) * SSD_GROUP_W)
        y_off = _dot(c_g.astype(BF16), state_ref[g].astype(BF16)) * decay_out[:, gcols]
        y_diag = []
        for pr in range(heads_per_group // 2):
            col0 = g * SSD_GROUP_W + pr * LANES
            x_pair = xs[:, col0:col0 + LANES].astype(BF16)
            halves = []
            for e in range(2):
                hidx = g * heads_per_group + 2 * pr + e
                diff = a_cs[:, hidx:hidx + 1] - a_cst[hidx:hidx + 1, :]
                wmat = cb * jnp.exp(jnp.where(tril, diff, -jnp.inf)) * dtt[hidx:hidx + 1, :]
                halves.append(_dot(wmat.astype(BF16), x_pair))
            y_diag.append(jnp.where(low, halves[0], halves[1]))
        ys.append(jnp.concatenate(y_diag, axis=1) + y_off)
        xw = (xs[:, gcols] * decay_in[:, gcols]).astype(BF16)
        state_ref[g] = state_ref[g] * chunk_decay[:, gcols] + _dot(b_g.T.astype(BF16), xw)

    y = jnp.concatenate(ys, axis=1) + xs * dskip_ref[...]
    y = y * _silu(z.astype(F32))
    outs = []
    for g in range(SSD_GROUPS):
        yg = y[:, g * SSD_GROUP_W:(g + 1) * SSD_GROUP_W]
        outs.append(yg * lax.rsqrt(jnp.mean(yg * yg, axis=-1, keepdims=True) + NORM_EPS))
    o_ref[...] = (jnp.concatenate(outs, axis=1) * nw_ref[...]).astype(BF16)


def _ssd(z, xbc, dt, dtt, sp, batch):
    m = z.shape[0]
    L = SSD_CHUNKS_PER_STEP * SSD_CHUNK
    nc = m // batch // L
    row = lambda w: pl.BlockSpec((L, w), lambda b, c: (b * nc + c, 0))
    full = _const_spec
    consts = (sp["conv_w"], sp["conv_b"], sp["dt_bias_row"], sp["dt_bias_col"], sp["a_log_row"],
              sp["a_log_col"], sp["d_skip"], sp["norm_w"], sp["expand"], sp["shift"])
    return pl.pallas_call(
        _ssd_kernel,
        out_shape=jax.ShapeDtypeStruct((m, SSD_D_INNER), BF16),
        grid=(batch, nc),
        in_specs=[row(SSD_D_INNER), row(SSD_XBC_W), row(DT_PAD),
                  pl.BlockSpec((SSD_HEADS, L), lambda b, c: (0, b * nc + c))] + [full(a) for a in consts],
        out_specs=row(SSD_D_INNER),
        scratch_shapes=[pltpu.VMEM((L + CONV_TAIL, SSD_XBC_W), BF16),
                        pltpu.VMEM((SSD_GROUPS, SSD_D_STATE, SSD_GROUP_W), F32)],
        compiler_params=_cparams(2),
        name="ssd",
    )(*_operands(z, xbc, dt, dtt, *consts))


def _mem_kv_kernel(mem_ref, nw_ref, w_ref, k_ref, v_ref):
    hm = _rms(mem_ref[...], nw_ref[...]).astype(BF16)
    k_ref[...] = _dot(hm, w_ref[:, :D_MODEL]).astype(BF16)
    v_ref[...] = _dot(hm, w_ref[:, D_MODEL:]).astype(BF16)


def _mem_kv(mem2d, norm_w, w_ckv):
    m = mem2d.shape[0]
    t = MEM_LEN
    row = pl.BlockSpec((t, D_MODEL), lambda i: (i, 0))
    full = _const_spec
    return pl.pallas_call(
        _mem_kv_kernel,
        out_shape=(jax.ShapeDtypeStruct((m, D_MODEL), BF16), jax.ShapeDtypeStruct((m, D_MODEL), BF16)),
        grid=(m // t,),
        in_specs=[row, full(norm_w), full(w_ckv)],
        out_specs=(row, row),
        compiler_params=_cparams(1),
        name="mem_kv",
    )(*_operands(mem2d, norm_w, w_ckv))


def _merge_kernel(x_ref, ya_ref, yb_ref, yc_ref, nmix_ref, wg_ref, bg_ref, wa_ref, wb_ref, wc_ref, wmix_ref,
                  ncross_ref, wcq_ref, kmem_ref, vmem_ref, wco_ref, o_ref):
    x = x_ref[...]
    h = _rms(x, nmix_ref[...]).astype(BF16)
    merged = None
    for br, (y_ref, w_ref) in enumerate(((ya_ref, wa_ref), (yb_ref, wb_ref), (yc_ref, wc_ref))):
        cols = slice(br * D_MODEL, (br + 1) * D_MODEL)
        gate = jax.nn.sigmoid(_dot(h, wg_ref[:, cols]) + bg_ref[:, cols])
        term = gate * _dot(y_ref[...], w_ref[...])
        merged = term if merged is None else merged + term
    x = x + _dot(merged.astype(BF16), wmix_ref[...])

    hq = _rms(x, ncross_ref[...]).astype(BF16)
    q = _dot(hq, wcq_ref[...]).astype(BF16)
    scale = X_HEAD_DIM ** -0.5
    outs = []
    for hh in range(X_HEADS):
        cols = slice(hh * X_HEAD_DIM, (hh + 1) * X_HEAD_DIM)
        s = _dot_nt(q[:, cols], kmem_ref[:, cols]) * scale
        p = jnp.exp(s - jnp.max(s, axis=-1, keepdims=True))
        denom = jnp.sum(p, axis=-1, keepdims=True)
        outs.append(_dot(p.astype(BF16), vmem_ref[:, cols]) / denom)
    o = jnp.concatenate(outs, axis=1).astype(BF16)
    o_ref[...] = x + _dot(o, wco_ref[...])


def _merge(x2d, y_a, y_b, y_c, kmem, vmem, lp, batch):
    m = x2d.shape[0]
    t = MLP_ROW_TILE
    n = m // t
    per_b = n // batch
    row = lambda w: pl.BlockSpec((t, w), lambda i: (i, 0))
    full = _const_spec
    memspec = pl.BlockSpec((MEM_LEN, D_MODEL), lambda i: (i // per_b, 0))
    return pl.pallas_call(
        _merge_kernel,
        out_shape=jax.ShapeDtypeStruct((m, D_MODEL), F32),
        grid=(n,),
        in_specs=[row(D_MODEL), row(SWA_Q_W), row(MOBA_W), row(SSD_D_INNER),
                  full(lp["norm_mix"]), full(lp["w_gate"]), full(lp["b_gate"]),
                  full(lp["w_br_swa"]), full(lp["w_br_moba"]), full(lp["w_br_ssd"]), full(lp["w_mix_out"]),
                  full(lp["norm_cross"]), full(lp["w_cq"]), memspec, memspec, full(lp["w_co"])],
        out_specs=row(D_MODEL),
        compiler_params=_cparams(1),
        name="merge_cross",
    )(*_operands(x2d, y_a, y_b, y_c, lp["norm_mix"], lp["w_gate"], lp["b_gate"], lp["w_br_swa"], lp["w_br_moba"],
                 lp["w_br_ssd"], lp["w_mix_out"], lp["norm_cross"], lp["w_cq"], kmem, vmem, lp["w_co"]))


def _mlp_kernel(final, x_ref, nw_ref, wup_ref, wdown_ref, fnw_ref, o_ref):
    x = x_ref[...]
    hf = _rms(x, nw_ref[...]).astype(BF16)
    chunk = D_MODEL
    acc = x
    for c in range(D_FF // chunk):
        cols = slice(c * chunk, (c + 1) * chunk)
        u = jnp.maximum(_dot(hf, wup_ref[:, cols]), 0.0)
        acc = acc + _dot((u * u).astype(BF16), wdown_ref[cols, :])
    if final:
        acc = _rms(acc, fnw_ref[...])
    o_ref[...] = acc


def _mlp(x2d, norm_w, w_up, w_down, final_norm_w, final):
    m = x2d.shape[0]
    t = MLP_ROW_TILE
    row = pl.BlockSpec((t, D_MODEL), lambda i: (i, 0))
    full = _const_spec
    return pl.pallas_call(
        functools.partial(_mlp_kernel, final),
        out_shape=jax.ShapeDtypeStruct((m, D_MODEL), F32),
        grid=(m // t,),
        in_specs=[row, full(norm_w), full(w_up), full(w_down), full(final_norm_w)],
        out_specs=row,
        compiler_params=_cparams(1),
        name="mlp_final" if final else "mlp",
    )(*_operands(x2d, norm_w, w_up, w_down, final_norm_w))


_MATMUL_WEIGHTS = ("w_in", "w_gate", "w_br_swa", "w_br_moba", "w_br_ssd", "w_mix_out", "w_cq", "w_ckv", "w_co",
                   "w_up", "w_down")
_ROW_PARAMS = ("norm_mix", "b_gate", "norm_cross", "norm_mem", "norm_mlp", "conv_b", "ssd_norm")


def _prep_params(p):
    depth = p["w_in"].shape[0]
    out = {k: p[k].astype(BF16) for k in _MATMUL_WEIGHTS}
    out.update({k: p[k].astype(F32).reshape(depth, 1, -1) for k in _ROW_PARAMS})
    out["w_dt"] = jnp.pad(out["w_in"][:, :, _C_DT:], ((0, 0), (0, 0), (0, DT_PAD - SSD_HEADS)))
    out["w_vt"] = jnp.swapaxes(out["w_in"][:, :, _C_VB:_C_Z], 1, 2)
    out["w_dtt"] = jnp.swapaxes(out["w_in"][:, :, _C_DT:], 1, 2)
    pad_row = lambda v: jnp.pad(v.astype(F32), ((0, 0), (0, DT_PAD - SSD_HEADS))).reshape(depth, 1, DT_PAD)
    col = lambda v: v.astype(F32).reshape(depth, SSD_HEADS, 1)
    out.update(conv_w=p["conv_w"].astype(F32), dt_bias_row=pad_row(p["dt_bias"]), dt_bias_col=col(p["dt_bias"]),
               a_log_row=pad_row(p["a_log"]), a_log_col=col(p["a_log"]),
               d_skip=jnp.repeat(p["d_skip"].astype(F32), SSD_HEAD_DIM, axis=1).reshape(depth, 1, SSD_D_INNER),
               swa_sinks=p["swa_sinks"].astype(F32))
    return out


def _ssd_constants():
    head_of_col = jnp.arange(SSD_D_INNER) // SSD_HEAD_DIM
    expand = (jnp.arange(DT_PAD)[:, None] == head_of_col[None, :]).astype(BF16)
    t_idx = jnp.arange(SSD_CHUNK)[None, :, None]
    j_idx = jnp.arange(SSD_CONV - 1)[:, None, None]
    col = jnp.arange(SSD_CHUNK + CONV_TAIL)[None, None, :]
    shift = (col == CONV_TAIL + t_idx - (SSD_CONV - 1) + j_idx).astype(BF16)
    return expand, shift.reshape((SSD_CONV - 1) * SSD_CHUNK, SSD_CHUNK + CONV_TAIL)


def _layer_params(l, sp, expand, shift):
    lp = {k: _LayerParam(v, l) for k, v in sp.items() if k != "swa_sinks"}
    lp["swa_sinks"] = sp["swa_sinks"][l]
    lp["ssd"] = {"conv_w": lp["conv_w"], "conv_b": lp["conv_b"], "dt_bias_row": lp["dt_bias_row"],
                 "dt_bias_col": lp["dt_bias_col"], "a_log_row": lp["a_log_row"], "a_log_col": lp["a_log_col"],
                 "d_skip": lp["d_skip"], "norm_w": lp["ssd_norm"], "expand": expand, "shift": shift}
    return lp


def kernel(x, mem, positions, norm_mix, w_in, w_gate, b_gate, swa_sinks, conv_w, conv_b, dt_bias, a_log, d_skip,
           ssd_norm, w_br_swa, w_br_moba, w_br_ssd, w_mix_out, norm_cross, norm_mem, w_cq, w_ckv, w_co, norm_mlp,
           w_up, w_down, final_norm):
    batch, seq, d = x.shape
    depth = w_in.shape[0]
    assert d == D_MODEL and seq % MOBA_BLOCK == 0 and mem.shape[1] == MEM_LEN
    params = dict(norm_mix=norm_mix, w_in=w_in, w_gate=w_gate, b_gate=b_gate, swa_sinks=swa_sinks, conv_w=conv_w,
                  conv_b=conv_b, dt_bias=dt_bias, a_log=a_log, d_skip=d_skip, ssd_norm=ssd_norm, w_br_swa=w_br_swa,
                  w_br_moba=w_br_moba, w_br_ssd=w_br_ssd, w_mix_out=w_mix_out, norm_cross=norm_cross,
                  norm_mem=norm_mem, w_cq=w_cq, w_ckv=w_ckv, w_co=w_co, norm_mlp=norm_mlp, w_up=w_up, w_down=w_down)
    x2d = x.reshape(batch * seq, d)
    mem2d = mem.reshape(batch * MEM_LEN, d)
    fnw = final_norm.reshape(1, d).astype(F32)
    cos_t, sin_t = _rope_tables(positions)
    stacked = _prep_params(params)
    expand, shift = _ssd_constants()
    for l in range(depth):
        lp = _layer_params(l, stacked, expand, shift)
        (q_a, k_a, v_a, q_b, k_b, kmean, vt_b, z, xbc, dt, dtt) = _proj_in(
            x2d, lp["norm_mix"], lp["w_in"], lp["w_vt"], lp["w_dt"], lp["w_dtt"], cos_t, sin_t, batch)
        y_a = _swa(q_a, k_a, v_a, lp["swa_sinks"], batch)
        y_b = _moba(q_b, k_b, vt_b, kmean, batch)
        y_c = _ssd(z, xbc, dt, dtt, lp["ssd"], batch)
        kmem, vmem = _mem_kv(mem2d, lp["norm_mem"], lp["w_ckv"])
        x2d = _merge(x2d, y_a, y_b, y_c, kmem, vmem, lp, batch)
        x2d = _mlp(x2d, lp["norm_mlp"], lp["w_up"], lp["w_down"], fnw, final=(l == depth - 1))
    return x2d.reshape(batch, seq, d)
```

```python
import functools
import math
from typing import NamedTuple

import jax
import jax.numpy as jnp
from jax import lax
from jax.experimental import pallas as pl
from jax.experimental.pallas import tpu as pltpu

F32 = jnp.float32
BF16 = jnp.bfloat16

D_MODEL = 1024
MEM_LEN = 256
NORM_EPS = 1e-6
SWA_Q_HEADS = 8
SWA_KV_HEADS = 2
SWA_HEAD_DIM = 64
SWA_BLOCK = 128
SWA_BLOCKS_PER_STEP = 4
ROPE_THETA = 150000.0
MOBA_HEADS = 8
MOBA_HEAD_DIM = 64
MOBA_BLOCK = 256
MOBA_TOPK = 3
MOBA_GROUP = 4
MOBA_SCORE_SPAN = 1
MOBA_PAIRS_PER_STEP = 4
SUM_ROWS = 16
MOBA_VT_ROWS = MOBA_HEADS * (MOBA_HEAD_DIM + SUM_ROWS)
SSD_D_INNER = D_MODEL
SSD_HEAD_DIM = 64
SSD_HEADS = SSD_D_INNER // SSD_HEAD_DIM
SSD_GROUPS = 2
SSD_D_STATE = 128
SSD_CONV = 4
SSD_CHUNK = 128
SSD_CHUNKS_PER_STEP = 4
X_HEADS = 4
X_HEAD_DIM = D_MODEL // X_HEADS
D_FF = 4 * D_MODEL
N_BRANCH = 3

SWA_Q_W = SWA_Q_HEADS * SWA_HEAD_DIM
SWA_KV_W = SWA_KV_HEADS * SWA_HEAD_DIM
MOBA_W = MOBA_HEADS * MOBA_HEAD_DIM
SSD_BC_W = SSD_GROUPS * SSD_D_STATE
SSD_XBC_W = SSD_D_INNER + 2 * SSD_BC_W
SSD_GROUP_W = SSD_D_INNER // SSD_GROUPS

LANES = 128
HEAD_PAIR_W = 2 * MOBA_HEAD_DIM
DT_PAD = LANES
CONV_TAIL = 16
ROW_TILE = 256
MLP_ROW_TILE = 512
VMEM_LIMIT = 56 * 1024 * 1024

_C_QA = 0
_C_KA = _C_QA + SWA_Q_W
_C_VA = _C_KA + SWA_KV_W
_C_QB = _C_VA + SWA_KV_W
_C_KB = _C_QB + MOBA_W
_C_VB = _C_KB + MOBA_W
_C_Z = _C_VB + MOBA_W
_C_XBC = _C_Z + SSD_D_INNER
_C_DT = _C_XBC + SSD_XBC_W
D_IN_PROJ = _C_DT + SSD_HEADS

LOG2E = math.log2(math.e)
_NT = (((1,), (1,)), ((), ()))


class _LayerParam(NamedTuple):
    stacked: jax.Array
    layer: int


def _const_spec(a):
    if isinstance(a, _LayerParam):
        shape = a.stacked.shape[1:]
        layer = a.layer
        return pl.BlockSpec((None,) + shape, lambda *_: (layer,) + (0,) * len(shape), pipeline_mode=pl.Buffered(1))
    return pl.BlockSpec(a.shape, lambda *_: (0,) * a.ndim, pipeline_mode=pl.Buffered(1))


def _operands(*args):
    return tuple(a.stacked if isinstance(a, _LayerParam) else a for a in args)


def _cparams(n_axes, flags=None):
    return pltpu.CompilerParams(dimension_semantics=("arbitrary",) * n_axes,
                                vmem_limit_bytes=VMEM_LIMIT, flags=flags)


def _rms(x, w):
    ms = jnp.mean(x * x, axis=-1, keepdims=True)
    return x * lax.rsqrt(ms + NORM_EPS) * w


def _dot(a, b):
    return jnp.dot(a, b, preferred_element_type=F32)


def _dot_nt(a, b, precision=None):
    return lax.dot_general(a, b, _NT, preferred_element_type=F32, precision=precision)


def _silu(x):
    half = 0.5 * x
    return half + half * jnp.tanh(half)


def _rope_table_kernel(pos_ref, inv_ref, cos_ref, sin_ref):
    ang = pos_ref[...].astype(F32) * inv_ref[...]
    lane = lax.broadcasted_iota(jnp.int32, ang.shape, 1)
    first_half = (lane % SWA_HEAD_DIM) < (SWA_HEAD_DIM // 2)
    cos_ref[...] = jnp.cos(ang)
    sin_ref[...] = jnp.where(first_half, -jnp.sin(ang), jnp.sin(ang))


def _rope_tables(positions):
    m = positions.size
    half = SWA_HEAD_DIM // 2
    inv = ROPE_THETA ** (-jnp.arange(half, dtype=F32) / half)
    inv = jnp.tile(inv, LANES // half).reshape(1, LANES)
    t = 1024 if m % 1024 == 0 else ROW_TILE
    return pl.pallas_call(
        _rope_table_kernel,
        out_shape=(jax.ShapeDtypeStruct((m, LANES), F32), jax.ShapeDtypeStruct((m, LANES), F32)),
        grid=(m // t,),
        in_specs=[pl.BlockSpec((t, 1), lambda i: (i, 0)), pl.BlockSpec((1, LANES), lambda i: (0, 0))],
        out_specs=(pl.BlockSpec((t, LANES), lambda i: (i, 0)), pl.BlockSpec((t, LANES), lambda i: (i, 0))),
        compiler_params=_cparams(1),
        name="rope_tables",
    )(positions.reshape(m, 1), inv)


def _rope(t, cos, sin):
    w = t.shape[-1]
    reps = w // LANES
    cos_w = jnp.concatenate([cos] * reps, axis=1)
    sin_w = jnp.concatenate([sin] * reps, axis=1)
    half = SWA_HEAD_DIM // 2
    lane = lax.broadcasted_iota(jnp.int32, t.shape, 1)
    first_half = (lane % SWA_HEAD_DIM) < half
    partner = jnp.where(first_half, pltpu.roll(t, w - half, 1), pltpu.roll(t, half, 1))
    return t * cos_w + partner * sin_w


def _dup_kv_heads(t):
    swapped = pltpu.roll(t, SWA_HEAD_DIM, 1)
    low = lax.broadcasted_iota(jnp.int32, t.shape, 1) < SWA_HEAD_DIM
    return jnp.concatenate([jnp.where(low, t, swapped), jnp.where(low, swapped, t)], axis=1)


def _proj_in_kernel(nkb, x_ref, nw_ref, w_ref, wvt_ref, wdtt_ref, cos_ref, sin_ref,
                    qa_ref, ka_ref, va_ref, qb_ref, kb_ref, kmean_ref, vt_ref,
                    z_ref, xbc_ref, dt_ref, dtt_ref):
    i = pl.program_id(0)
    hf = _rms(x_ref[...], nw_ref[...])
    h = hf.astype(BF16)
    ht = hf.T.astype(BF16)
    cos = cos_ref[...]
    sin = sin_ref[...]

    def seg(a, b):
        return _dot(h, w_ref[:, a:b])

    qa_ref[...] = _rope(seg(_C_QA, _C_KA), cos, sin).astype(BF16)
    ka_ref[...] = _dup_kv_heads(_rope(seg(_C_KA, _C_VA), cos, sin)).astype(BF16)
    va_ref[...] = _dup_kv_heads(seg(_C_VA, _C_QB)).astype(BF16)
    qb_ref[...] = seg(_C_QB, _C_KB).astype(BF16)
    kb = seg(_C_KB, _C_VB)
    kb_ref[...] = (kb * LOG2E).astype(BF16)
    per_tile = kb.shape[0] // MOBA_BLOCK
    for j in range(per_tile):
        blk_mean = jnp.mean(kb[j * MOBA_BLOCK:(j + 1) * MOBA_BLOCK], axis=0, keepdims=True)
        kmean_ref[0, pl.ds((i * per_tile + j) % nkb, 1), :] = blk_mean
    vt = _dot(wvt_ref[...], ht).astype(BF16)
    ext = MOBA_HEAD_DIM + SUM_ROWS
    for hh in range(MOBA_HEADS):
        vt_ref[hh * ext:hh * ext + MOBA_HEAD_DIM, :] = vt[hh * MOBA_HEAD_DIM:(hh + 1) * MOBA_HEAD_DIM]
        vt_ref[hh * ext + MOBA_HEAD_DIM:(hh + 1) * ext, :] = jnp.ones((SUM_ROWS, vt.shape[1]), BF16)
    z_ref[...] = seg(_C_Z, _C_XBC).astype(BF16)
    xbc_ref[...] = seg(_C_XBC, _C_DT).astype(BF16)
    dt_ref[...] = seg(_C_DT, _C_DT + DT_PAD)
    dtt_ref[...] = _dot(wdtt_ref[...], ht)


def _proj_in(x2d, norm_w, w_in, w_vt, w_dtt, cos_t, sin_t, batch):
    m = x2d.shape[0]
    t = MLP_ROW_TILE
    n = m // t
    nkb = m // batch // MOBA_BLOCK
    per_b = n // batch
    row = lambda w: pl.BlockSpec((t, w), lambda i: (i, 0))
    full = _const_spec
    out_shape = (
        jax.ShapeDtypeStruct((m, SWA_Q_W), BF16),
        jax.ShapeDtypeStruct((m, 2 * SWA_KV_W), BF16),
        jax.ShapeDtypeStruct((m, 2 * SWA_KV_W), BF16),
        jax.ShapeDtypeStruct((m, MOBA_W), BF16),
        jax.ShapeDtypeStruct((m, MOBA_W), BF16),
        jax.ShapeDtypeStruct((batch, nkb, MOBA_W), F32),
        jax.ShapeDtypeStruct((MOBA_VT_ROWS, m), BF16),
        jax.ShapeDtypeStruct((m, SSD_D_INNER), BF16),
        jax.ShapeDtypeStruct((m, SSD_XBC_W), BF16),
        jax.ShapeDtypeStruct((m, DT_PAD), F32),
        jax.ShapeDtypeStruct((SSD_HEADS, m), F32),
    )
    out_specs = (
        row(SWA_Q_W), row(2 * SWA_KV_W), row(2 * SWA_KV_W), row(MOBA_W), row(MOBA_W),
        pl.BlockSpec((1, nkb, MOBA_W), lambda i: (i // per_b, 0, 0)),
        pl.BlockSpec((MOBA_VT_ROWS, t), lambda i: (0, i)),
        row(SSD_D_INNER), row(SSD_XBC_W), row(DT_PAD),
        pl.BlockSpec((SSD_HEADS, t), lambda i: (0, i)),
    )
    return pl.pallas_call(
        functools.partial(_proj_in_kernel, nkb),
        out_shape=out_shape,
        grid=(n,),
        in_specs=[row(D_MODEL), full(norm_w), full(w_in), full(w_vt), full(w_dtt), row(LANES), row(LANES)],
        out_specs=out_specs,
        compiler_params=_cparams(1),
        name="proj_in",
    )(*_operands(x2d, norm_w, w_in, w_vt, w_dtt, cos_t, sin_t))


def _swa_kernel(sink_ref, q_ref, kp_ref, kc_ref, vp_ref, vc_ref, o_ref):
    i = pl.program_id(1)
    blk = SWA_BLOCK
    kall = jnp.concatenate([kp_ref[...], kc_ref[...]], axis=0)
    vall = jnp.concatenate([vp_ref[...], vc_ref[...]], axis=0)
    qi = lax.broadcasted_iota(jnp.int32, (blk, 2 * blk), 0)
    si = lax.broadcasted_iota(jnp.int32, (blk, 2 * blk), 1)
    delta = qi + blk - si
    in_window = (delta >= 0) & (delta < blk)
    lane = lax.broadcasted_iota(jnp.int32, (blk, LANES), 1)
    low = lane < SWA_HEAD_DIM
    group = SWA_Q_HEADS // SWA_KV_HEADS
    for sub in range(SWA_BLOCKS_PER_STEP):
        rows = slice(sub * blk, (sub + 1) * blk)
        mask = in_window & ((si >= blk) | (i > 0)) if sub == 0 else in_window
        q = q_ref[rows, :] * (SWA_HEAD_DIM ** -0.5)
        kcat = kall[sub * blk:(sub + 2) * blk]
        vcat = vall[sub * blk:(sub + 2) * blk]
        outs = []
        for hd in range(SWA_Q_HEADS):
            g = hd // group
            kd = kcat[:, g * LANES:(g + 1) * LANES]
            vd = vcat[:, g * LANES:(g + 1) * LANES]
            slab = q[:, (hd // 2) * LANES:(hd // 2 + 1) * LANES]
            qm = jnp.where(low if hd % 2 == 0 else ~low, slab, jnp.zeros_like(slab))
            s = jnp.where(mask, _dot_nt(qm, kd), -jnp.inf)
            sink = sink_ref[hd]
            mx = jnp.maximum(jnp.max(s, axis=-1, keepdims=True), sink)
            p = jnp.exp(s - mx)
            denom = jnp.sum(p, axis=-1, keepdims=True) + jnp.exp(sink - mx)
            outs.append(_dot(p.astype(BF16), vd) / denom)
        for pr in range(SWA_Q_HEADS // 2):
            o_ref[rows, pr * LANES:(pr + 1) * LANES] = jnp.where(low, outs[2 * pr], outs[2 * pr + 1]).astype(BF16)


def _swa(q_a, k_a, v_a, sinks, batch):
    m = q_a.shape[0]
    blk = SWA_BLOCK
    per = SWA_BLOCKS_PER_STEP
    nb = m // batch // blk
    steps = nb // per
    kvw = 2 * SWA_KV_W
    cur = lambda b, i: (b * steps + i, 0)
    prev = lambda b, i: (b * nb + jnp.maximum(per * i - 1, 0), 0)
    return pl.pallas_call(
        _swa_kernel,
        out_shape=jax.ShapeDtypeStruct((m, SWA_Q_W), BF16),
        grid=(batch, steps),
        in_specs=[pl.BlockSpec(memory_space=pltpu.SMEM),
                  pl.BlockSpec((per * blk, SWA_Q_W), cur),
                  pl.BlockSpec((blk, kvw), prev), pl.BlockSpec((per * blk, kvw), cur),
                  pl.BlockSpec((blk, kvw), prev), pl.BlockSpec((per * blk, kvw), cur)],
        out_specs=pl.BlockSpec((per * blk, SWA_Q_W), cur),
        compiler_params=_cparams(2),
        name="swa",
    )(sinks, q_a, k_a, k_a, v_a, v_a)


def _fold8(x, op):
    return functools.reduce(op, [x[r:r + 8] for r in range(0, x.shape[0], 8)])


def _moba_kernel(q_ref, k_ref, vt_ref, kmean_ref, o_ref, *scratch):
    i = pl.program_id(2)
    per_pipe = len(scratch) // MOBA_PAIRS_PER_STEP
    pipes = []
    for ps in range(MOBA_PAIRS_PER_STEP):
        lanes = pl.ds(ps * HEAD_PAIR_W, HEAD_PAIR_W)
        vt_rows = pl.ds(ps * 2 * (MOBA_HEAD_DIM + SUM_ROWS), 2 * (MOBA_HEAD_DIM + SUM_ROWS))
        pipes.append(_moba_pipeline(i, q_ref.at[:, lanes], k_ref.at[:, lanes], vt_ref.at[vt_rows, :],
                                    kmean_ref.at[:, :, lanes], o_ref.at[:, lanes],
                                    *scratch[ps * per_pipe:(ps + 1) * per_pipe]))
    n_groups = (i + MOBA_GROUP - 1) // MOBA_GROUP

    for p in pipes:
        p["setup"]()
    for p in pipes:
        p["scores_own"]()
    mx0 = [None] * len(pipes)
    for r in range(0, MOBA_GROUP, MOBA_SCORE_SPAN):
        for n, p in enumerate(pipes):
            mx0[n] = p["score_blocks"](0, 0, r, mx0[n])
            if r == 0:
                p["accumulate_own"]()
    for n, p in enumerate(pipes):
        p["end_scores"](0, mx0[n])

    def stage(score_g, score_buf, acc_g, acc_buf):
        state = [p["begin"](acc_buf) for p in pipes]
        mx = [None] * len(pipes)
        for r in range(0, MOBA_GROUP, MOBA_SCORE_SPAN):
            for n, p in enumerate(pipes):
                mx[n] = p["score_blocks"](score_g, score_buf, r, mx[n])
                for rr in range(r, r + MOBA_SCORE_SPAN):
                    p["exp_block"](acc_g, acc_buf, rr, state[n])
        for n, p in enumerate(pipes):
            p["end_scores"](score_buf, mx[n])

    def pair(u, carry):
        g = 2 * u
        stage(g + 1, 1, g, 0)
        stage(g + 2, 0, g + 1, 1)
        return carry

    lax.fori_loop(0, n_groups // 2, pair, 0)

    @pl.when(n_groups % 2 == 1)
    def _():
        for p in pipes:
            p["accumulate"](n_groups - 1, 0)

    for p in pipes:
        p["finalize"]()


def _moba_pipeline(i, q_ref, k_ref, vt_ref, kmean_ref, o_ref, qm_ref, sel_ref, ml_ref, acc_ref,
                   so_ref, mxo_ref, po_ref, sa_ref, mxa_ref, pa_ref, sb_ref, mxb_ref, pb_ref):
    blk = MOBA_BLOCK
    hd = MOBA_HEAD_DIM
    grp = MOBA_GROUP
    nkb = kmean_ref.shape[1]
    stage_bufs = ((sa_ref, mxa_ref, pa_ref), (sb_ref, mxb_ref, pb_ref))

    def setup():
        qt = (q_ref[...].astype(F32) * (hd ** -0.5)).T
        row = lax.broadcasted_iota(jnp.int32, qt.shape, 0)
        qm_ref[:, 0:blk] = jnp.where(row < hd, qt, 0.0).astype(BF16)
        qm_ref[:, blk:2 * blk] = jnp.where(row >= hd, qt, 0.0).astype(BF16)

        blk_id = lax.broadcasted_iota(jnp.int32, (nkb, 2 * blk), 0)
        gate = None
        resid = kmean_ref[0]
        for _ in range(3):
            part = resid.astype(BF16)
            resid = resid - part.astype(F32)
            term = _dot(part, qm_ref[...])
            gate = term if gate is None else gate + term
        gate = jnp.where(blk_id < i, gate, -jnp.inf)
        sel = jnp.zeros((nkb, 2 * blk), F32)
        for _ in range(min(MOBA_TOPK, nkb)):
            best = jnp.max(gate, axis=0, keepdims=True)
            idx = jnp.min(jnp.where(gate == best, blk_id, nkb), axis=0, keepdims=True)
            pick = blk_id == idx
            sel = jnp.where(pick & (blk_id < i), 1.0, sel)
            gate = jnp.where(pick, -jnp.inf, gate)
        sel_ref[0:nkb, :] = sel
        sel_ref[nkb:nkb + grp, :] = jnp.zeros((grp, 2 * blk), F32)

    def score_blocks(start, r, keeps, s_buf, mx):
        n = len(keeps)
        s_all = _dot(k_ref[pl.ds(start + r * blk, n * blk), :], qm_ref[...])
        for j, keep in enumerate(keeps):
            s = jnp.where(keep, s_all[j * blk:(j + 1) * blk], -jnp.inf)
            s_buf[(r + j) * blk:(r + j + 1) * blk, :] = s
            m8 = _fold8(s, jnp.maximum)
            mx = m8 if mx is None else jnp.maximum(mx, m8)
        return mx

    def begin(mx_buf, first):
        mx = jnp.max(mx_buf[...], axis=0, keepdims=True)
        if first:
            m_new, alpha = mx, None
        else:
            m_old = ml_ref[0:1, :]
            m_new = jnp.maximum(m_old, mx)
            alpha = jnp.exp2(m_old - m_new)
            for h in range(2):
                acc_ref[h] = alpha[:, h * blk:(h + 1) * blk] * acc_ref[h]
        ml_ref[0:1, :] = m_new
        return m_new, alpha

    def exp_block(s_buf, p_buf, r, m_new):
        p_buf[r * blk:(r + 1) * blk, :] = jnp.exp2(s_buf[r * blk:(r + 1) * blk, :] - m_new).astype(BF16)

    def pv_block(start, r, p_buf, assign):
        ext = hd + SUM_ROWS
        for h in range(2):
            vt_ext = vt_ref[h * ext:(h + 1) * ext, pl.ds(start + r * blk, blk)]
            term = _dot(vt_ext, p_buf[r * blk:(r + 1) * blk, h * blk:(h + 1) * blk])
            acc_ref[h] = term if assign else acc_ref[h] + term

    def group_start(g):
        return pl.multiple_of(jnp.minimum(g, nkb // grp - 1) * (grp * blk), grp * blk)

    def group_score_blocks(g, buf, r, mx):
        keeps = [sel_ref[pl.ds(g * grp + r + j, 1), :] > 0.0 for j in range(MOBA_SCORE_SPAN)]
        return score_blocks(group_start(g), r, keeps, stage_bufs[buf][0], mx)

    def group_end_scores(buf, mx):
        stage_bufs[buf][1][...] = mx

    def group_begin(buf):
        return begin(stage_bufs[buf][1], False)

    def group_exp_block(g, buf, r, state):
        exp_block(stage_bufs[buf][0], stage_bufs[buf][2], r, state[0])
        pv_block(group_start(g), r, stage_bufs[buf][2], False)

    def group_accumulate(g, buf):
        state = group_begin(buf)
        for r in range(grp):
            group_exp_block(g, buf, r, state)

    own = pl.multiple_of(i * blk, blk)

    def scores_own():
        kpos = lax.broadcasted_iota(jnp.int32, (blk, 2 * blk), 0)
        qpos = lax.broadcasted_iota(jnp.int32, (blk, 2 * blk), 1) % blk
        mxo_ref[...] = score_blocks(own, 0, [kpos <= qpos], so_ref, None)

    def accumulate_own():
        m_new, _ = begin(mxo_ref, True)
        exp_block(so_ref, po_ref, 0, m_new)
        pv_block(own, 0, po_ref, True)

    def finalize():
        out_t = jnp.concatenate([acc_ref[h, 0:hd, :] / acc_ref[h, hd:hd + 1, :] for h in range(2)], axis=0)
        o_ref[...] = out_t.T.astype(BF16)

    return dict(setup=setup, scores_own=scores_own, accumulate_own=accumulate_own,
                accumulate=group_accumulate, finalize=finalize, begin=group_begin,
                score_blocks=group_score_blocks, exp_block=group_exp_block, end_scores=group_end_scores)


def _moba_stage(n_blocks, blk):
    return [pltpu.VMEM((n_blocks * blk, 2 * blk), F32), pltpu.VMEM((8, 2 * blk), F32),
            pltpu.VMEM((n_blocks * blk, 2 * blk), BF16)]


def _moba(q_b, k_b, vt_b, kmean, batch):
    m = q_b.shape[0]
    s = m // batch
    blk = MOBA_BLOCK
    nkb = s // blk
    assert nkb % MOBA_GROUP == 0
    w = MOBA_PAIRS_PER_STEP * HEAD_PAIR_W
    steps = MOBA_W // w
    pipe_scratch = [pltpu.VMEM((HEAD_PAIR_W, 2 * blk), BF16),
                    pltpu.VMEM((nkb + MOBA_GROUP, 2 * blk), F32),
                    pltpu.VMEM((8, 2 * blk), F32),
                    pltpu.VMEM((2, MOBA_HEAD_DIM + SUM_ROWS, blk), F32),
                    ] + _moba_stage(1, blk) + _moba_stage(MOBA_GROUP, blk) + _moba_stage(MOBA_GROUP, blk)
    return pl.pallas_call(
        _moba_kernel,
        out_shape=jax.ShapeDtypeStruct((m, MOBA_W), BF16),
        grid=(batch, steps, nkb),
        in_specs=[pl.BlockSpec((blk, w), lambda b, p, i: (b * nkb + i, p)),
                  pl.BlockSpec((s, w), lambda b, p, i: (b, p), pipeline_mode=pl.Buffered(1)),
                  pl.BlockSpec((MOBA_VT_ROWS // steps, s), lambda b, p, i: (p, b), pipeline_mode=pl.Buffered(1)),
                  pl.BlockSpec((1, nkb, w), lambda b, p, i: (b, 0, p))],
        out_specs=pl.BlockSpec((blk, w), lambda b, p, i: (b * nkb + i, p)),
        scratch_shapes=pipe_scratch * MOBA_PAIRS_PER_STEP,
        compiler_params=_cparams(3),
        name="moba",
    )(q_b, k_b, vt_b, kmean)


def _split_dot(v, e):
    hi = v.astype(BF16)
    lo = (v - hi.astype(F32)).astype(BF16)
    return _dot(hi, e) + _dot(lo, e)


def _ssd_kernel(z_ref, xbc_ref, dt_ref, dtt_ref, cw_ref, cb_ref, dtb_ref, dtbc_ref, alog_ref, alogc_ref,
                dskip_ref, nw_ref, exp_ref, shift_ref, o_ref, ext_ref, state_ref):
    c = pl.program_id(1)
    L = SSD_CHUNK
    tail = CONV_TAIL

    last = SSD_CHUNKS_PER_STEP * L

    @pl.when(c > 0)
    def _():
        ext_ref[0:tail, :] = ext_ref[last:last + tail, :]

    @pl.when(c == 0)
    def _():
        ext_ref[0:tail, :] = jnp.zeros((tail, SSD_XBC_W), BF16)
        state_ref[...] = jnp.zeros_like(state_ref)

    ext_ref[tail:tail + last, :] = xbc_ref[...]
    for sub in range(SSD_CHUNKS_PER_STEP):
        rows = slice(sub * L, (sub + 1) * L)
        window = ext_ref[sub * L:(sub + 1) * L + tail, :]
        _ssd_chunk(window, xbc_ref[rows, :], z_ref[rows, :], dt_ref[rows, :], dtt_ref[:, rows], cw_ref, cb_ref,
                   dtb_ref, dtbc_ref, alog_ref, alogc_ref, dskip_ref, nw_ref, exp_ref, shift_ref,
                   o_ref.at[rows, :], state_ref)


def _ssd_chunk(window, xbc, z, dt_raw, dtt_raw, cw_ref, cb_ref, dtb_ref, dtbc_ref, alog_ref, alogc_ref,
               dskip_ref, nw_ref, exp_ref, shift_ref, o_ref, state_ref):
    L = SSD_CHUNK
    shifted = _dot(shift_ref[...], window)
    acc = cb_ref[...] + cw_ref[SSD_CONV - 1:SSD_CONV, :] * xbc.astype(F32)
    for j in range(SSD_CONV - 1):
        acc = acc + cw_ref[j:j + 1, :] * shifted[j * L:(j + 1) * L, :]
    xc = _silu(acc)
    xs = xc[:, :SSD_D_INNER]
    bm = xc[:, SSD_D_INNER:SSD_D_INNER + SSD_BC_W]
    cm = xc[:, SSD_D_INNER + SSD_BC_W:]

    dt = jax.nn.softplus(dt_raw + dtb_ref[...])
    dtt = jax.nn.softplus(dtt_raw + dtbc_ref[...])
    a_row = -jnp.exp(alog_ref[...]) * LOG2E
    a_col = -jnp.exp(alogc_ref[...]) * LOG2E
    r_i = lax.broadcasted_iota(jnp.int32, (L, L), 0)
    c_i = lax.broadcasted_iota(jnp.int32, (L, L), 1)
    tril = r_i >= c_i
    hp = lax.Precision.HIGHEST
    a_cs = jnp.dot(tril.astype(F32), dt * a_row, precision=hp, preferred_element_type=F32)
    a_cst = jnp.dot(dtt * a_col, (r_i <= c_i).astype(F32), precision=hp, preferred_element_type=F32)
    a_src = a_cst - jnp.log2(dtt)
    a_last = a_cs[L - 1:L, :]
    expand = exp_ref[...]
    decay_out = _split_dot(jnp.exp2(a_cs), expand)
    decay_in = _split_dot(dt * jnp.exp2(a_last - a_cs), expand)
    chunk_decay = decay_out[L - 1:L, :]

    lane = lax.broadcasted_iota(jnp.int32, (L, LANES), 1)
    low = lane < SSD_HEAD_DIM
    heads_per_group = SSD_HEADS // SSD_GROUPS
    ys = []
    for g in range(SSD_GROUPS):
        b_g = bm[:, g * SSD_D_STATE:(g + 1) * SSD_D_STATE]
        c_g = cm[:, g * SSD_D_STATE:(g + 1) * SSD_D_STATE]
        cb = _dot_nt(c_g.astype(BF16), b_g.astype(BF16))
        gcols = slice(g * SSD_GROUP_W, (g + 1) * SSD_GROUP_W)
        y_off = _dot(c_g.astype(BF16), state_ref[g].astype(BF16)) * decay_out[:, gcols]
        y_diag = []
        for pr in range(heads_per_group // 2):
            col0 = g * SSD_GROUP_W + pr * LANES
            x_pair = xs[:, col0:col0 + LANES].astype(BF16)
            halves = []
            for e in range(2):
                hidx = g * heads_per_group + 2 * pr + e
                diff = a_cs[:, hidx:hidx + 1] - a_src[hidx:hidx + 1, :]
                wmat = cb * jnp.exp2(jnp.where(tril, diff, -jnp.inf))
                halves.append(_dot(wmat.astype(BF16), x_pair))
            y_diag.append(jnp.where(low, halves[0], halves[1]))
        ys.append(jnp.concatenate(y_diag, axis=1) + y_off)
        xw = (xs[:, gcols] * decay_in[:, gcols]).astype(BF16)
        state_ref[g] = state_ref[g] * chunk_decay[:, gcols] + _dot(b_g.T.astype(BF16), xw)

    y = jnp.concatenate(ys, axis=1) + xs * dskip_ref[...]
    y = y * _silu(z.astype(F32))
    outs = []
    for g in range(SSD_GROUPS):
        yg = y[:, g * SSD_GROUP_W:(g + 1) * SSD_GROUP_W]
        outs.append(yg * lax.rsqrt(jnp.mean(yg * yg, axis=-1, keepdims=True) + NORM_EPS))
    o_ref[...] = (jnp.concatenate(outs, axis=1) * nw_ref[...]).astype(BF16)


def _ssd(z, xbc, dt, dtt, sp, batch):
    m = z.shape[0]
    L = SSD_CHUNKS_PER_STEP * SSD_CHUNK
    nc = m // batch // L
    row = lambda w: pl.BlockSpec((L, w), lambda b, c: (b * nc + c, 0))
    full = _const_spec
    consts = (sp["conv_w"], sp["conv_b"], sp["dt_bias_row"], sp["dt_bias_col"], sp["a_log_row"],
              sp["a_log_col"], sp["d_skip"], sp["norm_w"], sp["expand"], sp["shift"])
    return pl.pallas_call(
        _ssd_kernel,
        out_shape=jax.ShapeDtypeStruct((m, SSD_D_INNER), BF16),
        grid=(batch, nc),
        in_specs=[row(SSD_D_INNER), row(SSD_XBC_W), row(DT_PAD),
                  pl.BlockSpec((SSD_HEADS, L), lambda b, c: (0, b * nc + c))] + [full(a) for a in consts],
        out_specs=row(SSD_D_INNER),
        scratch_shapes=[pltpu.VMEM((L + CONV_TAIL, SSD_XBC_W), BF16),
                        pltpu.VMEM((SSD_GROUPS, SSD_D_STATE, SSD_GROUP_W), F32)],
        compiler_params=_cparams(2),
        name="ssd",
    )(*_operands(z, xbc, dt, dtt, *consts))


def _mem_kv_kernel(mem_ref, nw_ref, wkt_ref, w_ref, kt_ref, v_ref):
    hm = _rms(mem_ref[...], nw_ref[...])
    kt_ref[...] = _dot(wkt_ref[...], hm.T.astype(BF16)).astype(BF16)
    v_ref[...] = _dot(hm.astype(BF16), w_ref[:, D_MODEL:]).astype(BF16)


def _mem_kv(mem2d, norm_w, w_ckt, w_ckv):
    m = mem2d.shape[0]
    t = MEM_LEN
    row = pl.BlockSpec((t, D_MODEL), lambda i: (i, 0))
    full = _const_spec
    return pl.pallas_call(
        _mem_kv_kernel,
        out_shape=(jax.ShapeDtypeStruct((m // t * D_MODEL, t), BF16), jax.ShapeDtypeStruct((m, D_MODEL), BF16)),
        grid=(m // t,),
        in_specs=[row, full(norm_w), full(w_ckt), full(w_ckv)],
        out_specs=(pl.BlockSpec((D_MODEL, t), lambda i: (i, 0)), row),
        compiler_params=_cparams(1),
        name="mem_kv",
    )(*_operands(mem2d, norm_w, w_ckt, w_ckv))


def _merge_kernel(final, x_ref, ya_ref, yb_ref, yc_ref, nmix_ref, wg_ref, bg_ref, wa_ref, wb_ref, wc_ref, wmix_ref,
                  ncross_ref, wcq_ref, ktmem_ref, vmem_ref, wco_ref, nmlp_ref, wup_ref, wdown_ref, fnw_ref, o_ref):
    x = x_ref[...]
    h = _rms(x, nmix_ref[...]).astype(BF16)
    merged = None
    for br, (y_ref, w_ref) in enumerate(((ya_ref, wa_ref), (yb_ref, wb_ref), (yc_ref, wc_ref))):
        cols = slice(br * D_MODEL, (br + 1) * D_MODEL)
        gate = jax.nn.sigmoid(_dot(h, wg_ref[:, cols]) + bg_ref[:, cols])
        term = gate * _dot(y_ref[...], w_ref[...])
        merged = term if merged is None else merged + term
    x = x + _dot(merged.astype(BF16), wmix_ref[...])

    hq = _rms(x, ncross_ref[...]).astype(BF16)
    q = _dot(hq, wcq_ref[...]).astype(BF16)
    scale = X_HEAD_DIM ** -0.5
    outs = []
    for hh in range(X_HEADS):
        cols = slice(hh * X_HEAD_DIM, (hh + 1) * X_HEAD_DIM)
        s = _dot(q[:, cols], ktmem_ref[cols, :]) * scale
        p = jnp.exp(s - jnp.max(s, axis=-1, keepdims=True))
        denom = jnp.sum(p, axis=-1, keepdims=True)
        outs.append(_dot(p.astype(BF16), vmem_ref[:, cols]) / denom)
    o = jnp.concatenate(outs, axis=1).astype(BF16)
    x = x + _dot(o, wco_ref[...])

    hf = _rms(x, nmlp_ref[...]).astype(BF16)
    chunk = D_MODEL
    for c in range(D_FF // chunk):
        cols = slice(c * chunk, (c + 1) * chunk)
        u = jnp.maximum(_dot(hf, wup_ref[:, cols]), 0.0)
        x = x + _dot((u * u).astype(BF16), wdown_ref[cols, :])
    if final:
        x = _rms(x, fnw_ref[...])
    o_ref[...] = x


def _merge(x2d, y_a, y_b, y_c, ktmem, vmem, lp, fnw, batch, final):
    m = x2d.shape[0]
    t = MLP_ROW_TILE
    n = m // t
    per_b = n // batch
    row = lambda w: pl.BlockSpec((t, w), lambda i: (i, 0))
    full = _const_spec
    memspec = pl.BlockSpec((MEM_LEN, D_MODEL), lambda i: (i // per_b, 0))
    ktspec = pl.BlockSpec((D_MODEL, MEM_LEN), lambda i: (i // per_b, 0))
    consts_a = (lp["norm_mix"], lp["w_gate"], lp["b_gate"], lp["w_br_swa"], lp["w_br_moba"], lp["w_br_ssd"],
                lp["w_mix_out"], lp["norm_cross"], lp["w_cq"])
    consts_b = (lp["w_co"], lp["norm_mlp"], lp["w_up"], lp["w_down"], fnw)
    return pl.pallas_call(
        functools.partial(_merge_kernel, final),
        out_shape=jax.ShapeDtypeStruct((m, D_MODEL), F32),
        grid=(n,),
        in_specs=[row(D_MODEL), row(SWA_Q_W), row(MOBA_W), row(SSD_D_INNER)] + [full(a) for a in consts_a]
                 + [ktspec, memspec] + [full(a) for a in consts_b],
        out_specs=row(D_MODEL),
        compiler_params=_cparams(1),
        name="merge_mlp_final" if final else "merge_mlp",
    )(*_operands(x2d, y_a, y_b, y_c, *consts_a, ktmem, vmem, *consts_b))


_MATMUL_WEIGHTS = ("w_gate", "w_br_swa", "w_br_moba", "w_br_ssd", "w_mix_out", "w_cq", "w_ckv", "w_co",
                   "w_up", "w_down")
_ROW_PARAMS = ("norm_mix", "b_gate", "norm_cross", "norm_mem", "norm_mlp", "conv_b", "ssd_norm")


def _prep_params(p):
    depth = p["w_in"].shape[0]
    out = {k: p[k].astype(BF16) for k in _MATMUL_WEIGHTS}
    out.update({k: p[k].astype(F32).reshape(depth, 1, -1) for k in _ROW_PARAMS})
    out["w_in"] = jnp.pad(p["w_in"].astype(BF16), ((0, 0), (0, 0), (0, DT_PAD - SSD_HEADS)))
    out["w_ckt"] = jnp.swapaxes(out["w_ckv"][:, :, :D_MODEL], 1, 2)
    out["w_vt"] = jnp.swapaxes(out["w_in"][:, :, _C_VB:_C_Z], 1, 2)
    out["w_dtt"] = jnp.swapaxes(out["w_in"][:, :, _C_DT:D_IN_PROJ], 1, 2)
    pad_row = lambda v: jnp.pad(v.astype(F32), ((0, 0), (0, DT_PAD - SSD_HEADS))).reshape(depth, 1, DT_PAD)
    col = lambda v: v.astype(F32).reshape(depth, SSD_HEADS, 1)
    out.update(conv_w=p["conv_w"].astype(F32), dt_bias_row=pad_row(p["dt_bias"]), dt_bias_col=col(p["dt_bias"]),
               a_log_row=pad_row(p["a_log"]), a_log_col=col(p["a_log"]),
               d_skip=jnp.repeat(p["d_skip"].astype(F32), SSD_HEAD_DIM, axis=1).reshape(depth, 1, SSD_D_INNER),
               swa_sinks=p["swa_sinks"].astype(F32))
    return out


def _ssd_constants():
    head_of_col = jnp.arange(SSD_D_INNER) // SSD_HEAD_DIM
    expand = (jnp.arange(DT_PAD)[:, None] == head_of_col[None, :]).astype(BF16)
    t_idx = jnp.arange(SSD_CHUNK)[None, :, None]
    j_idx = jnp.arange(SSD_CONV - 1)[:, None, None]
    col = jnp.arange(SSD_CHUNK + CONV_TAIL)[None, None, :]
    shift = (col == CONV_TAIL + t_idx - (SSD_CONV - 1) + j_idx).astype(BF16)
    return expand, shift.reshape((SSD_CONV - 1) * SSD_CHUNK, SSD_CHUNK + CONV_TAIL)


def _layer_params(l, sp, expand, shift):
    lp = {k: _LayerParam(v, l) for k, v in sp.items() if k != "swa_sinks"}
    lp["swa_sinks"] = sp["swa_sinks"][l]
    lp["ssd"] = {"conv_w": lp["conv_w"], "conv_b": lp["conv_b"], "dt_bias_row": lp["dt_bias_row"],
                 "dt_bias_col": lp["dt_bias_col"], "a_log_row": lp["a_log_row"], "a_log_col": lp["a_log_col"],
                 "d_skip": lp["d_skip"], "norm_w": lp["ssd_norm"], "expand": expand, "shift": shift}
    return lp


def kernel(x, mem, positions, norm_mix, w_in, w_gate, b_gate, swa_sinks, conv_w, conv_b, dt_bias, a_log, d_skip,
           ssd_norm, w_br_swa, w_br_moba, w_br_ssd, w_mix_out, norm_cross, norm_mem, w_cq, w_ckv, w_co, norm_mlp,
           w_up, w_down, final_norm):
    batch, seq, d = x.shape
    depth = w_in.shape[0]
    assert d == D_MODEL and seq % MOBA_BLOCK == 0 and mem.shape[1] == MEM_LEN
    params = dict(norm_mix=norm_mix, w_in=w_in, w_gate=w_gate, b_gate=b_gate, swa_sinks=swa_sinks, conv_w=conv_w,
                  conv_b=conv_b, dt_bias=dt_bias, a_log=a_log, d_skip=d_skip, ssd_norm=ssd_norm, w_br_swa=w_br_swa,
                  w_br_moba=w_br_moba, w_br_ssd=w_br_ssd, w_mix_out=w_mix_out, norm_cross=norm_cross,
                  norm_mem=norm_mem, w_cq=w_cq, w_ckv=w_ckv, w_co=w_co, norm_mlp=norm_mlp, w_up=w_up, w_down=w_down)
    x2d = x.reshape(batch * seq, d)
    mem2d = mem.reshape(batch * MEM_LEN, d)
    fnw = final_norm.reshape(1, d).astype(F32)
    cos_t, sin_t = _rope_tables(positions)
    stacked = _prep_params(params)
    expand, shift = _ssd_constants()
    for l in range(depth):
        lp = _layer_params(l, stacked, expand, shift)
        (q_a, k_a, v_a, q_b, k_b, kmean, vt_b, z, xbc, dt, dtt) = _proj_in(
            x2d, lp["norm_mix"], lp["w_in"], lp["w_vt"], lp["w_dtt"], cos_t, sin_t, batch)
        y_a = _swa(q_a, k_a, v_a, lp["swa_sinks"], batch)
        y_b = _moba(q_b, k_b, vt_b, kmean, batch)
        y_c = _ssd(z, xbc, dt, dtt, lp["ssd"], batch)
        ktmem, vmem = _mem_kv(mem2d, lp["norm_mem"], lp["w_ckt"], lp["w_ckv"])
        x2d = _merge(x2d, y_a, y_b, y_c, ktmem, vmem, lp, fnw, batch, final=(l == depth - 1))
    return x2d.reshape(batch, seq, d)
```

```python
import functools
import math
from typing import NamedTuple

import jax
import jax.numpy as jnp
from jax import lax
from jax.experimental import pallas as pl
from jax.experimental.pallas import tpu as pltpu

F32 = jnp.float32
BF16 = jnp.bfloat16

D_MODEL = 1024
MEM_LEN = 256
NORM_EPS = 1e-6
SWA_Q_HEADS = 8
SWA_KV_HEADS = 2
SWA_HEAD_DIM = 64
SWA_BLOCK = 128
SWA_BLOCKS_PER_STEP = 8
ROPE_THETA = 150000.0
MOBA_HEADS = 8
MOBA_HEAD_DIM = 64
MOBA_BLOCK = 256
MOBA_TOPK = 3
MOBA_GROUP = 2
MOBA_SCORE_SPAN = 1
MOBA_PAIRS_PER_STEP = 4
SUM_ROWS = 16
MOBA_VT_ROWS = MOBA_HEADS * (MOBA_HEAD_DIM + SUM_ROWS)
SSD_D_INNER = D_MODEL
SSD_HEAD_DIM = 64
SSD_HEADS = SSD_D_INNER // SSD_HEAD_DIM
SSD_GROUPS = 2
SSD_D_STATE = 128
SSD_CONV = 4
SSD_CHUNK = 128
SSD_CHUNKS_PER_STEP = 8
X_HEADS = 4
X_HEAD_DIM = D_MODEL // X_HEADS
D_FF = 4 * D_MODEL
N_BRANCH = 3

SWA_Q_W = SWA_Q_HEADS * SWA_HEAD_DIM
SWA_KV_W = SWA_KV_HEADS * SWA_HEAD_DIM
MOBA_W = MOBA_HEADS * MOBA_HEAD_DIM
SSD_BC_W = SSD_GROUPS * SSD_D_STATE
SSD_XBC_W = SSD_D_INNER + 2 * SSD_BC_W
SSD_GROUP_W = SSD_D_INNER // SSD_GROUPS

LANES = 128
HEAD_PAIR_W = 2 * MOBA_HEAD_DIM
DT_PAD = LANES
CONV_TAIL = 16
ROW_TILE = 256
MLP_ROW_TILE = 512
VMEM_LIMIT = 56 * 1024 * 1024

_C_QA = 0
_C_KA = _C_QA + SWA_Q_W
_C_VA = _C_KA + SWA_KV_W
_C_QB = _C_VA + SWA_KV_W
_C_KB = _C_QB + MOBA_W
_C_VB = _C_KB + MOBA_W
_C_Z = _C_VB + MOBA_W
_C_XBC = _C_Z + SSD_D_INNER
_C_DT = _C_XBC + SSD_XBC_W
D_IN_PROJ = _C_DT + SSD_HEADS

LOG2E = math.log2(math.e)
_NT = (((1,), (1,)), ((), ()))


class _LayerParam(NamedTuple):
    stacked: jax.Array
    layer: int


def _const_spec(a):
    if isinstance(a, _LayerParam):
        shape = a.stacked.shape[1:]
        layer = a.layer
        return pl.BlockSpec((None,) + shape, lambda *_: (layer,) + (0,) * len(shape), pipeline_mode=pl.Buffered(1))
    return pl.BlockSpec(a.shape, lambda *_: (0,) * a.ndim, pipeline_mode=pl.Buffered(1))


def _operands(*args):
    return tuple(a.stacked if isinstance(a, _LayerParam) else a for a in args)


def _cparams(n_axes, flags=None):
    return pltpu.CompilerParams(dimension_semantics=("arbitrary",) * n_axes,
                                vmem_limit_bytes=VMEM_LIMIT, flags=flags)


def _rms(x, w):
    ms = jnp.mean(x * x, axis=-1, keepdims=True)
    return x * lax.rsqrt(ms + NORM_EPS) * w


def _dot(a, b):
    return jnp.dot(a, b, preferred_element_type=F32)


def _dot_nt(a, b, precision=None):
    return lax.dot_general(a, b, _NT, preferred_element_type=F32, precision=precision)


def _silu(x):
    half = 0.5 * x
    return half + half * jnp.tanh(half)


def _rope_table_kernel(pos_ref, inv_ref, cos_ref, sin_ref):
    ang = pos_ref[...].astype(F32) * inv_ref[...]
    lane = lax.broadcasted_iota(jnp.int32, ang.shape, 1)
    first_half = (lane % SWA_HEAD_DIM) < (SWA_HEAD_DIM // 2)
    cos_ref[...] = jnp.cos(ang)
    sin_ref[...] = jnp.where(first_half, -jnp.sin(ang), jnp.sin(ang))


def _rope_tables(positions):
    m = positions.size
    half = SWA_HEAD_DIM // 2
    inv = ROPE_THETA ** (-jnp.arange(half, dtype=F32) / half)
    inv = jnp.tile(inv, LANES // half).reshape(1, LANES)
    t = 1024 if m % 1024 == 0 else ROW_TILE
    return pl.pallas_call(
        _rope_table_kernel,
        out_shape=(jax.ShapeDtypeStruct((m, LANES), F32), jax.ShapeDtypeStruct((m, LANES), F32)),
        grid=(m // t,),
        in_specs=[pl.BlockSpec((t, 1), lambda i: (i, 0)), pl.BlockSpec((1, LANES), lambda i: (0, 0))],
        out_specs=(pl.BlockSpec((t, LANES), lambda i: (i, 0)), pl.BlockSpec((t, LANES), lambda i: (i, 0))),
        compiler_params=_cparams(1),
        name="rope_tables",
    )(positions.reshape(m, 1), inv)


def _rope(t, cos, sin):
    w = t.shape[-1]
    reps = w // LANES
    cos_w = jnp.concatenate([cos] * reps, axis=1)
    sin_w = jnp.concatenate([sin] * reps, axis=1)
    half = SWA_HEAD_DIM // 2
    lane = lax.broadcasted_iota(jnp.int32, t.shape, 1)
    first_half = (lane % SWA_HEAD_DIM) < half
    partner = jnp.where(first_half, pltpu.roll(t, w - half, 1), pltpu.roll(t, half, 1))
    return t * cos_w + partner * sin_w


def _dup_kv_heads(t):
    swapped = pltpu.roll(t, SWA_HEAD_DIM, 1)
    low = lax.broadcasted_iota(jnp.int32, t.shape, 1) < SWA_HEAD_DIM
    return jnp.concatenate([jnp.where(low, t, swapped), jnp.where(low, swapped, t)], axis=1)


def _proj_in_kernel(nkb, x_ref, nw_ref, w_ref, wvt_ref, wdtt_ref, cos_ref, sin_ref,
                    qa_ref, ka_ref, va_ref, qb_ref, kb_ref, kmean_ref, vt_ref,
                    z_ref, xbc_ref, dt_ref, dtt_ref):
    i = pl.program_id(0)
    hf = _rms(x_ref[...], nw_ref[...])
    h = hf.astype(BF16)
    ht = hf.T.astype(BF16)
    cos = cos_ref[...]
    sin = sin_ref[...]

    def seg(a, b):
        return _dot(h, w_ref[:, a:b])

    qa_ref[...] = _rope(seg(_C_QA, _C_KA), cos, sin).astype(BF16)
    ka_ref[...] = _dup_kv_heads(_rope(seg(_C_KA, _C_VA), cos, sin)).astype(BF16)
    va_ref[...] = _dup_kv_heads(seg(_C_VA, _C_QB)).astype(BF16)
    qb_ref[...] = seg(_C_QB, _C_KB).astype(BF16)
    kb = seg(_C_KB, _C_VB)
    kb_ref[...] = (kb * LOG2E).astype(BF16)
    per_tile = kb.shape[0] // MOBA_BLOCK
    for j in range(per_tile):
        blk_mean = jnp.mean(kb[j * MOBA_BLOCK:(j + 1) * MOBA_BLOCK], axis=0, keepdims=True)
        kmean_ref[0, pl.ds((i * per_tile + j) % nkb, 1), :] = blk_mean
    vt = _dot(wvt_ref[...], ht).astype(BF16)
    ext = MOBA_HEAD_DIM + SUM_ROWS
    for hh in range(MOBA_HEADS):
        vt_ref[hh * ext:hh * ext + MOBA_HEAD_DIM, :] = vt[hh * MOBA_HEAD_DIM:(hh + 1) * MOBA_HEAD_DIM]
        vt_ref[hh * ext + MOBA_HEAD_DIM:(hh + 1) * ext, :] = jnp.ones((SUM_ROWS, vt.shape[1]), BF16)
    z_ref[...] = seg(_C_Z, _C_XBC).astype(BF16)
    xbc_ref[...] = seg(_C_XBC, _C_DT).astype(BF16)
    dt_ref[...] = seg(_C_DT, _C_DT + DT_PAD)
    dtt_ref[...] = _dot(wdtt_ref[...], ht)


def _proj_in(x2d, norm_w, w_in, w_vt, w_dtt, cos_t, sin_t, batch):
    m = x2d.shape[0]
    t = MLP_ROW_TILE
    n = m // t
    nkb = m // batch // MOBA_BLOCK
    per_b = n // batch
    row = lambda w: pl.BlockSpec((t, w), lambda i: (i, 0))
    full = _const_spec
    out_shape = (
        jax.ShapeDtypeStruct((m, SWA_Q_W), BF16),
        jax.ShapeDtypeStruct((m, 2 * SWA_KV_W), BF16),
        jax.ShapeDtypeStruct((m, 2 * SWA_KV_W), BF16),
        jax.ShapeDtypeStruct((m, MOBA_W), BF16),
        jax.ShapeDtypeStruct((m, MOBA_W), BF16),
        jax.ShapeDtypeStruct((batch, nkb, MOBA_W), F32),
        jax.ShapeDtypeStruct((MOBA_VT_ROWS, m), BF16),
        jax.ShapeDtypeStruct((m, SSD_D_INNER), BF16),
        jax.ShapeDtypeStruct((m, SSD_XBC_W), BF16),
        jax.ShapeDtypeStruct((m, DT_PAD), F32),
        jax.ShapeDtypeStruct((SSD_HEADS, m), F32),
    )
    out_specs = (
        row(SWA_Q_W), row(2 * SWA_KV_W), row(2 * SWA_KV_W), row(MOBA_W), row(MOBA_W),
        pl.BlockSpec((1, nkb, MOBA_W), lambda i: (i // per_b, 0, 0)),
        pl.BlockSpec((MOBA_VT_ROWS, t), lambda i: (0, i)),
        row(SSD_D_INNER), row(SSD_XBC_W), row(DT_PAD),
        pl.BlockSpec((SSD_HEADS, t), lambda i: (0, i)),
    )
    return pl.pallas_call(
        functools.partial(_proj_in_kernel, nkb),
        out_shape=out_shape,
        grid=(n,),
        in_specs=[row(D_MODEL), full(norm_w), full(w_in), full(w_vt), full(w_dtt), row(LANES), row(LANES)],
        out_specs=out_specs,
        compiler_params=_cparams(1),
        name="proj_in",
    )(*_operands(x2d, norm_w, w_in, w_vt, w_dtt, cos_t, sin_t))


def _swa_kernel(sink_ref, q_ref, kp_ref, kc_ref, vp_ref, vc_ref, o_ref):
    i = pl.program_id(1)
    blk = SWA_BLOCK
    kall = jnp.concatenate([kp_ref[...], kc_ref[...]], axis=0)
    vall = jnp.concatenate([vp_ref[...], vc_ref[...]], axis=0)
    qi = lax.broadcasted_iota(jnp.int32, (blk, 2 * blk), 0)
    si = lax.broadcasted_iota(jnp.int32, (blk, 2 * blk), 1)
    delta = qi + blk - si
    in_window = (delta >= 0) & (delta < blk)
    lane = lax.broadcasted_iota(jnp.int32, (blk, LANES), 1)
    low = lane < SWA_HEAD_DIM
    group = SWA_Q_HEADS // SWA_KV_HEADS
    for sub in range(SWA_BLOCKS_PER_STEP):
        rows = slice(sub * blk, (sub + 1) * blk)
        mask = in_window & ((si >= blk) | (i > 0)) if sub == 0 else in_window
        q = q_ref[rows, :] * (SWA_HEAD_DIM ** -0.5)
        kcat = kall[sub * blk:(sub + 2) * blk]
        vcat = vall[sub * blk:(sub + 2) * blk]
        outs = []
        for hd in range(SWA_Q_HEADS):
            g = hd // group
            kd = kcat[:, g * LANES:(g + 1) * LANES]
            vd = vcat[:, g * LANES:(g + 1) * LANES]
            slab = q[:, (hd // 2) * LANES:(hd // 2 + 1) * LANES]
            qm = jnp.where(low if hd % 2 == 0 else ~low, slab, jnp.zeros_like(slab))
            s = jnp.where(mask, _dot_nt(qm, kd), -jnp.inf)
            sink = sink_ref[hd]
            mx = jnp.maximum(jnp.max(s, axis=-1, keepdims=True), sink)
            p = jnp.exp(s - mx)
            denom = jnp.sum(p, axis=-1, keepdims=True) + jnp.exp(sink - mx)
            outs.append(_dot(p.astype(BF16), vd) / denom)
        for pr in range(SWA_Q_HEADS // 2):
            o_ref[rows, pr * LANES:(pr + 1) * LANES] = jnp.where(low, outs[2 * pr], outs[2 * pr + 1]).astype(BF16)


def _swa(q_a, k_a, v_a, sinks, batch):
    m = q_a.shape[0]
    blk = SWA_BLOCK
    per = SWA_BLOCKS_PER_STEP
    nb = m // batch // blk
    steps = nb // per
    kvw = 2 * SWA_KV_W
    cur = lambda b, i: (b * steps + i, 0)
    prev = lambda b, i: (b * nb + jnp.maximum(per * i - 1, 0), 0)
    return pl.pallas_call(
        _swa_kernel,
        out_shape=jax.ShapeDtypeStruct((m, SWA_Q_W), BF16),
        grid=(batch, steps),
        in_specs=[pl.BlockSpec(memory_space=pltpu.SMEM),
                  pl.BlockSpec((per * blk, SWA_Q_W), cur),
                  pl.BlockSpec((blk, kvw), prev), pl.BlockSpec((per * blk, kvw), cur),
                  pl.BlockSpec((blk, kvw), prev), pl.BlockSpec((per * blk, kvw), cur)],
        out_specs=pl.BlockSpec((per * blk, SWA_Q_W), cur),
        compiler_params=_cparams(2),
        name="swa",
    )(sinks, q_a, k_a, k_a, v_a, v_a)


def _fold8(x, op):
    return functools.reduce(op, [x[r:r + 8] for r in range(0, x.shape[0], 8)])


def _moba_kernel(q_ref, k_ref, vt_ref, kmean_ref, o_ref, *scratch):
    i = pl.program_id(2)
    per_pipe = len(scratch) // MOBA_PAIRS_PER_STEP
    pipes = []
    for ps in range(MOBA_PAIRS_PER_STEP):
        lanes = pl.ds(ps * HEAD_PAIR_W, HEAD_PAIR_W)
        vt_rows = pl.ds(ps * 2 * (MOBA_HEAD_DIM + SUM_ROWS), 2 * (MOBA_HEAD_DIM + SUM_ROWS))
        pipes.append(_moba_pipeline(i, q_ref.at[:, lanes], k_ref.at[:, lanes], vt_ref.at[vt_rows, :],
                                    kmean_ref.at[:, :, lanes], o_ref.at[:, lanes],
                                    *scratch[ps * per_pipe:(ps + 1) * per_pipe]))
    n_groups = (i + MOBA_GROUP - 1) // MOBA_GROUP

    for p in pipes:
        p["setup"]()
    for p in pipes:
        p["scores_own"]()
    mx0 = [None] * len(pipes)
    for r in range(0, MOBA_GROUP, MOBA_SCORE_SPAN):
        for n, p in enumerate(pipes):
            mx0[n] = p["score_blocks"](0, 0, r, mx0[n])
            if r == 0:
                p["accumulate_own"]()
    for n, p in enumerate(pipes):
        p["end_scores"](0, mx0[n])

    def stage(score_g, score_buf, acc_g, acc_buf):
        state = [p["begin"](acc_buf) for p in pipes]
        mx = [None] * len(pipes)
        for r in range(0, MOBA_GROUP, MOBA_SCORE_SPAN):
            for n, p in enumerate(pipes):
                mx[n] = p["score_blocks"](score_g, score_buf, r, mx[n])
                for rr in range(r, r + MOBA_SCORE_SPAN):
                    p["exp_block"](acc_g, acc_buf, rr, state[n])
        for n, p in enumerate(pipes):
            p["end_scores"](score_buf, mx[n])

    def pair(u, carry):
        g = 2 * u
        stage(g + 1, 1, g, 0)
        stage(g + 2, 0, g + 1, 1)
        return carry

    lax.fori_loop(0, n_groups // 2, pair, 0)

    @pl.when(n_groups % 2 == 1)
    def _():
        for p in pipes:
            p["accumulate"](n_groups - 1, 0)

    for p in pipes:
        p["finalize"]()


def _moba_pipeline(i, q_ref, k_ref, vt_ref, kmean_ref, o_ref, qm_ref, sel_ref, ml_ref, acc_ref,
                   so_ref, mxo_ref, po_ref, sa_ref, mxa_ref, pa_ref, sb_ref, mxb_ref, pb_ref):
    blk = MOBA_BLOCK
    hd = MOBA_HEAD_DIM
    grp = MOBA_GROUP
    nkb = kmean_ref.shape[1]
    stage_bufs = ((sa_ref, mxa_ref, pa_ref), (sb_ref, mxb_ref, pb_ref))

    def setup():
        qt = (q_ref[...].astype(F32) * (hd ** -0.5)).T
        row = lax.broadcasted_iota(jnp.int32, qt.shape, 0)
        qm_ref[:, 0:blk] = jnp.where(row < hd, qt, 0.0).astype(BF16)
        qm_ref[:, blk:2 * blk] = jnp.where(row >= hd, qt, 0.0).astype(BF16)

        blk_id = lax.broadcasted_iota(jnp.int32, (nkb, 2 * blk), 0)
        gate = None
        resid = kmean_ref[0]
        for _ in range(3):
            part = resid.astype(BF16)
            resid = resid - part.astype(F32)
            term = _dot(part, qm_ref[...])
            gate = term if gate is None else gate + term
        gate = jnp.where(blk_id < i, gate, -jnp.inf)
        sel = jnp.zeros((nkb, 2 * blk), F32)
        for _ in range(min(MOBA_TOPK, nkb)):
            best = jnp.max(gate, axis=0, keepdims=True)
            idx = jnp.min(jnp.where(gate == best, blk_id, nkb), axis=0, keepdims=True)
            pick = blk_id == idx
            sel = jnp.where(pick & (blk_id < i), 1.0, sel)
            gate = jnp.where(pick, -jnp.inf, gate)
        sel_ref[0:nkb, :] = sel
        sel_ref[nkb:nkb + grp, :] = jnp.zeros((grp, 2 * blk), F32)

    def score_blocks(start, r, keeps, s_buf, mx):
        n = len(keeps)
        s_all = _dot(k_ref[pl.ds(start + r * blk, n * blk), :], qm_ref[...])
        for j, keep in enumerate(keeps):
            s = jnp.where(keep, s_all[j * blk:(j + 1) * blk], -jnp.inf)
            s_buf[(r + j) * blk:(r + j + 1) * blk, :] = s
            m8 = _fold8(s, jnp.maximum)
            mx = m8 if mx is None else jnp.maximum(mx, m8)
        return mx

    def begin(mx_buf, first):
        mx = jnp.max(mx_buf[...], axis=0, keepdims=True)
        if first:
            m_new, alpha = mx, None
        else:
            m_old = ml_ref[0:1, :]
            m_new = jnp.maximum(m_old, mx)
            alpha = jnp.exp2(m_old - m_new)
            for h in range(2):
                acc_ref[h] = alpha[:, h * blk:(h + 1) * blk] * acc_ref[h]
        ml_ref[0:1, :] = m_new
        return m_new, alpha

    def exp_block(s_buf, p_buf, r, m_new):
        p_buf[r * blk:(r + 1) * blk, :] = jnp.exp2(s_buf[r * blk:(r + 1) * blk, :] - m_new).astype(BF16)

    def pv_block(start, r, p_buf, assign):
        ext = hd + SUM_ROWS
        for h in range(2):
            vt_ext = vt_ref[h * ext:(h + 1) * ext, pl.ds(start + r * blk, blk)]
            term = _dot(vt_ext, p_buf[r * blk:(r + 1) * blk, h * blk:(h + 1) * blk])
            acc_ref[h] = term if assign else acc_ref[h] + term

    def group_start(g):
        return pl.multiple_of(jnp.minimum(g, nkb // grp - 1) * (grp * blk), grp * blk)

    def group_score_blocks(g, buf, r, mx):
        keeps = [sel_ref[pl.ds(g * grp + r + j, 1), :] > 0.0 for j in range(MOBA_SCORE_SPAN)]
        return score_blocks(group_start(g), r, keeps, stage_bufs[buf][0], mx)

    def group_end_scores(buf, mx):
        stage_bufs[buf][1][...] = mx

    def group_begin(buf):
        return begin(stage_bufs[buf][1], False)

    def group_exp_block(g, buf, r, state):
        exp_block(stage_bufs[buf][0], stage_bufs[buf][2], r, state[0])
        pv_block(group_start(g), r, stage_bufs[buf][2], False)

    def group_accumulate(g, buf):
        state = group_begin(buf)
        for r in range(grp):
            group_exp_block(g, buf, r, state)

    own = pl.multiple_of(i * blk, blk)

    def scores_own():
        kpos = lax.broadcasted_iota(jnp.int32, (blk, 2 * blk), 0)
        qpos = lax.broadcasted_iota(jnp.int32, (blk, 2 * blk), 1) % blk
        mxo_ref[...] = score_blocks(own, 0, [kpos <= qpos], so_ref, None)

    def accumulate_own():
        m_new, _ = begin(mxo_ref, True)
        exp_block(so_ref, po_ref, 0, m_new)
        pv_block(own, 0, po_ref, True)

    def finalize():
        out_t = jnp.concatenate([acc_ref[h, 0:hd, :] / acc_ref[h, hd:hd + 1, :] for h in range(2)], axis=0)
        o_ref[...] = out_t.T.astype(BF16)

    return dict(setup=setup, scores_own=scores_own, accumulate_own=accumulate_own,
                accumulate=group_accumulate, finalize=finalize, begin=group_begin,
                score_blocks=group_score_blocks, exp_block=group_exp_block, end_scores=group_end_scores)


def _moba_stage(n_blocks, blk):
    return [pltpu.VMEM((n_blocks * blk, 2 * blk), F32), pltpu.VMEM((8, 2 * blk), F32),
            pltpu.VMEM((n_blocks * blk, 2 * blk), BF16)]


def _moba(q_b, k_b, vt_b, kmean, batch):
    m = q_b.shape[0]
    s = m // batch
    blk = MOBA_BLOCK
    nkb = s // blk
    assert nkb % MOBA_GROUP == 0
    w = MOBA_PAIRS_PER_STEP * HEAD_PAIR_W
    steps = MOBA_W // w
    pipe_scratch = [pltpu.VMEM((HEAD_PAIR_W, 2 * blk), BF16),
                    pltpu.VMEM((nkb + MOBA_GROUP, 2 * blk), F32),
                    pltpu.VMEM((8, 2 * blk), F32),
                    pltpu.VMEM((2, MOBA_HEAD_DIM + SUM_ROWS, blk), F32),
                    ] + _moba_stage(1, blk) + _moba_stage(MOBA_GROUP, blk) + _moba_stage(MOBA_GROUP, blk)
    return pl.pallas_call(
        _moba_kernel,
        out_shape=jax.ShapeDtypeStruct((m, MOBA_W), BF16),
        grid=(batch, steps, nkb),
        in_specs=[pl.BlockSpec((blk, w), lambda b, p, i: (b * nkb + i, p)),
                  pl.BlockSpec((s, w), lambda b, p, i: (b, p), pipeline_mode=pl.Buffered(1)),
                  pl.BlockSpec((MOBA_VT_ROWS // steps, s), lambda b, p, i: (p, b), pipeline_mode=pl.Buffered(1)),
                  pl.BlockSpec((1, nkb, w), lambda b, p, i: (b, 0, p))],
        out_specs=pl.BlockSpec((blk, w), lambda b, p, i: (b * nkb + i, p)),
        scratch_shapes=pipe_scratch * MOBA_PAIRS_PER_STEP,
        compiler_params=_cparams(3),
        name="moba",
    )(q_b, k_b, vt_b, kmean)


def _split_dot(v, e):
    hi = v.astype(BF16)
    lo = (v - hi.astype(F32)).astype(BF16)
    return _dot(hi, e) + _dot(lo, e)


def _ssd_kernel(z_ref, xbc_ref, dt_ref, dtt_ref, cw_ref, cb_ref, dtb_ref, dtbc_ref, alog_ref, alogc_ref,
                dskip_ref, nw_ref, exp_ref, shift_ref, o_ref, ext_ref, state_ref):
    c = pl.program_id(1)
    L = SSD_CHUNK
    tail = CONV_TAIL

    last = SSD_CHUNKS_PER_STEP * L

    @pl.when(c > 0)
    def _():
        ext_ref[0:tail, :] = ext_ref[last:last + tail, :]

    @pl.when(c == 0)
    def _():
        ext_ref[0:tail, :] = jnp.zeros((tail, SSD_XBC_W), BF16)
        state_ref[...] = jnp.zeros_like(state_ref)

    ext_ref[tail:tail + last, :] = xbc_ref[...]
    for sub in range(SSD_CHUNKS_PER_STEP):
        rows = slice(sub * L, (sub + 1) * L)
        window = ext_ref[sub * L:(sub + 1) * L + tail, :]
        _ssd_chunk(window, xbc_ref[rows, :], z_ref[rows, :], dt_ref[rows, :], dtt_ref[:, rows], cw_ref, cb_ref,
                   dtb_ref, dtbc_ref, alog_ref, alogc_ref, dskip_ref, nw_ref, exp_ref, shift_ref,
                   o_ref.at[rows, :], state_ref)


def _ssd_chunk(window, xbc, z, dt_raw, dtt_raw, cw_ref, cb_ref, dtb_ref, dtbc_ref, alog_ref, alogc_ref,
               dskip_ref, nw_ref, exp_ref, shift_ref, o_ref, state_ref):
    L = SSD_CHUNK
    shifted = _dot(shift_ref[...], window)
    acc = cb_ref[...] + cw_ref[SSD_CONV - 1:SSD_CONV, :] * xbc.astype(F32)
    for j in range(SSD_CONV - 1):
        acc = acc + cw_ref[j:j + 1, :] * shifted[j * L:(j + 1) * L, :]
    xc = _silu(acc)
    xs = xc[:, :SSD_D_INNER]
    bm = xc[:, SSD_D_INNER:SSD_D_INNER + SSD_BC_W]
    cm = xc[:, SSD_D_INNER + SSD_BC_W:]

    dt = jax.nn.softplus(dt_raw + dtb_ref[...])
    dtt = jax.nn.softplus(dtt_raw + dtbc_ref[...])
    a_row = -jnp.exp(alog_ref[...]) * LOG2E
    a_col = -jnp.exp(alogc_ref[...]) * LOG2E
    r_i = lax.broadcasted_iota(jnp.int32, (L, L), 0)
    c_i = lax.broadcasted_iota(jnp.int32, (L, L), 1)
    tril = r_i >= c_i
    hp = lax.Precision.HIGHEST
    a_cs = jnp.dot(tril.astype(F32), dt * a_row, precision=hp, preferred_element_type=F32)
    a_cst = jnp.dot(dtt * a_col, (r_i <= c_i).astype(F32), precision=hp, preferred_element_type=F32)
    a_src = a_cst - jnp.log2(dtt)
    a_last = a_cs[L - 1:L, :]
    expand = exp_ref[...]
    decay_out = _split_dot(jnp.exp2(a_cs), expand)
    decay_in = _split_dot(dt * jnp.exp2(a_last - a_cs), expand)
    chunk_decay = decay_out[L - 1:L, :]

    lane = lax.broadcasted_iota(jnp.int32, (L, LANES), 1)
    low = lane < SSD_HEAD_DIM
    heads_per_group = SSD_HEADS // SSD_GROUPS
    ys = []
    for g in range(SSD_GROUPS):
        b_g = bm[:, g * SSD_D_STATE:(g + 1) * SSD_D_STATE]
        c_g = cm[:, g * SSD_D_STATE:(g + 1) * SSD_D_STATE]
        cb = _dot_nt(c_g.astype(BF16), b_g.astype(BF16))
        gcols = slice(g * SSD_GROUP_W, (g + 1) * SSD_GROUP_W)
        y_off = _dot(c_g.astype(BF16), state_ref[g].astype(BF16)) * decay_out[:, gcols]
        y_diag = []
        for pr in range(heads_per_group // 2):
            col0 = g * SSD_GROUP_W + pr * LANES
            x_pair = xs[:, col0:col0 + LANES].astype(BF16)
            halves = []
            for e in range(2):
                hidx = g * heads_per_group + 2 * pr + e
                diff = a_cs[:, hidx:hidx + 1] - a_src[hidx:hidx + 1, :]
                wmat = cb * jnp.exp2(jnp.where(tril, diff, -jnp.inf))
                halves.append(_dot(wmat.astype(BF16), x_pair))
            y_diag.append(jnp.where(low, halves[0], halves[1]))
        ys.append(jnp.concatenate(y_diag, axis=1) + y_off)
        xw = (xs[:, gcols] * decay_in[:, gcols]).astype(BF16)
        state_ref[g] = state_ref[g] * chunk_decay[:, gcols] + _dot(b_g.T.astype(BF16), xw)

    y = jnp.concatenate(ys, axis=1) + xs * dskip_ref[...]
    y = y * _silu(z.astype(F32))
    outs = []
    for g in range(SSD_GROUPS):
        yg = y[:, g * SSD_GROUP_W:(g + 1) * SSD_GROUP_W]
        outs.append(yg * lax.rsqrt(jnp.mean(yg * yg, axis=-1, keepdims=True) + NORM_EPS))
    o_ref[...] = (jnp.concatenate(outs, axis=1) * nw_ref[...]).astype(BF16)


def _ssd(z, xbc, dt, dtt, sp, batch):
    m = z.shape[0]
    L = SSD_CHUNKS_PER_STEP * SSD_CHUNK
    nc = m // batch // L
    row = lambda w: pl.BlockSpec((L, w), lambda b, c: (b * nc + c, 0))
    full = _const_spec
    consts = (sp["conv_w"], sp["conv_b"], sp["dt_bias_row"], sp["dt_bias_col"], sp["a_log_row"],
              sp["a_log_col"], sp["d_skip"], sp["norm_w"], sp["expand"], sp["shift"])
    return pl.pallas_call(
        _ssd_kernel,
        out_shape=jax.ShapeDtypeStruct((m, SSD_D_INNER), BF16),
        grid=(batch, nc),
        in_specs=[row(SSD_D_INNER), row(SSD_XBC_W), row(DT_PAD),
                  pl.BlockSpec((SSD_HEADS, L), lambda b, c: (0, b * nc + c))] + [full(a) for a in consts],
        out_specs=row(SSD_D_INNER),
        scratch_shapes=[pltpu.VMEM((L + CONV_TAIL, SSD_XBC_W), BF16),
                        pltpu.VMEM((SSD_GROUPS, SSD_D_STATE, SSD_GROUP_W), F32)],
        compiler_params=_cparams(2),
        name="ssd",
    )(*_operands(z, xbc, dt, dtt, *consts))


def _mem_kv_kernel(mem_ref, nw_ref, wkt_ref, w_ref, kt_ref, v_ref):
    hm = _rms(mem_ref[...], nw_ref[...])
    kt_ref[...] = _dot(wkt_ref[...], hm.T.astype(BF16)).astype(BF16)
    v_ref[...] = _dot(hm.astype(BF16), w_ref[:, D_MODEL:]).astype(BF16)


def _mem_kv(mem2d, norm_w, w_ckt, w_ckv):
    m = mem2d.shape[0]
    t = MEM_LEN
    row = pl.BlockSpec((t, D_MODEL), lambda i: (i, 0))
    full = _const_spec
    return pl.pallas_call(
        _mem_kv_kernel,
        out_shape=(jax.ShapeDtypeStruct((m // t * D_MODEL, t), BF16), jax.ShapeDtypeStruct((m, D_MODEL), BF16)),
        grid=(m // t,),
        in_specs=[row, full(norm_w), full(w_ckt), full(w_ckv)],
        out_specs=(pl.BlockSpec((D_MODEL, t), lambda i: (i, 0)), row),
        compiler_params=_cparams(1),
        name="mem_kv",
    )(*_operands(mem2d, norm_w, w_ckt, w_ckv))


def _merge_kernel(final, x_ref, ya_ref, yb_ref, yc_ref, nmix_ref, wg_ref, bg_ref, wa_ref, wb_ref, wc_ref, wmix_ref,
                  ncross_ref, wcq_ref, ktmem_ref, vmem_ref, wco_ref, nmlp_ref, wup_ref, wdown_ref, fnw_ref, o_ref):
    x = x_ref[...]
    h = _rms(x, nmix_ref[...]).astype(BF16)
    merged = None
    for br, (y_ref, w_ref) in enumerate(((ya_ref, wa_ref), (yb_ref, wb_ref), (yc_ref, wc_ref))):
        cols = slice(br * D_MODEL, (br + 1) * D_MODEL)
        gate = jax.nn.sigmoid(_dot(h, wg_ref[:, cols]) + bg_ref[:, cols])
        term = gate * _dot(y_ref[...], w_ref[...])
        merged = term if merged is None else merged + term
    x = x + _dot(merged.astype(BF16), wmix_ref[...])

    hq = _rms(x, ncross_ref[...]).astype(BF16)
    q = _dot(hq, wcq_ref[...]).astype(BF16)
    scale = X_HEAD_DIM ** -0.5
    outs = []
    for hh in range(X_HEADS):
        cols = slice(hh * X_HEAD_DIM, (hh + 1) * X_HEAD_DIM)
        s = _dot(q[:, cols], ktmem_ref[cols, :]) * scale
        p = jnp.exp(s - jnp.max(s, axis=-1, keepdims=True))
        denom = jnp.sum(p, axis=-1, keepdims=True)
        outs.append(_dot(p.astype(BF16), vmem_ref[:, cols]) / denom)
    o = jnp.concatenate(outs, axis=1).astype(BF16)
    x = x + _dot(o, wco_ref[...])

    hf = _rms(x, nmlp_ref[...]).astype(BF16)
    chunk = D_MODEL
    for c in range(D_FF // chunk):
        cols = slice(c * chunk, (c + 1) * chunk)
        u = jnp.maximum(_dot(hf, wup_ref[:, cols]), 0.0)
        x = x + _dot((u * u).astype(BF16), wdown_ref[cols, :])
    if final:
        x = _rms(x, fnw_ref[...])
    o_ref[...] = x


def _merge(x2d, y_a, y_b, y_c, ktmem, vmem, lp, fnw, batch, final):
    m = x2d.shape[0]
    t = MLP_ROW_TILE
    n = m // t
    per_b = n // batch
    row = lambda w: pl.BlockSpec((t, w), lambda i: (i, 0))
    full = _const_spec
    memspec = pl.BlockSpec((MEM_LEN, D_MODEL), lambda i: (i // per_b, 0))
    ktspec = pl.BlockSpec((D_MODEL, MEM_LEN), lambda i: (i // per_b, 0))
    consts_a = (lp["norm_mix"], lp["w_gate"], lp["b_gate"], lp["w_br_swa"], lp["w_br_moba"], lp["w_br_ssd"],
                lp["w_mix_out"], lp["norm_cross"], lp["w_cq"])
    consts_b = (lp["w_co"], lp["norm_mlp"], lp["w_up"], lp["w_down"], fnw)
    return pl.pallas_call(
        functools.partial(_merge_kernel, final),
        out_shape=jax.ShapeDtypeStruct((m, D_MODEL), F32),
        grid=(n,),
        in_specs=[row(D_MODEL), row(SWA_Q_W), row(MOBA_W), row(SSD_D_INNER)] + [full(a) for a in consts_a]
                 + [ktspec, memspec] + [full(a) for a in consts_b],
        out_specs=row(D_MODEL),
        compiler_params=_cparams(1),
        name="merge_mlp_final" if final else "merge_mlp",
    )(*_operands(x2d, y_a, y_b, y_c, *consts_a, ktmem, vmem, *consts_b))


_MATMUL_WEIGHTS = ("w_gate", "w_br_swa", "w_br_moba", "w_br_ssd", "w_mix_out", "w_cq", "w_ckv", "w_co",
                   "w_up", "w_down")
_ROW_PARAMS = ("norm_mix", "b_gate", "norm_cross", "norm_mem", "norm_mlp", "conv_b", "ssd_norm")


def _prep_params(p):
    depth = p["w_in"].shape[0]
    out = {k: p[k].astype(BF16) for k in _MATMUL_WEIGHTS}
    out.update({k: p[k].astype(F32).reshape(depth, 1, -1) for k in _ROW_PARAMS})
    out["w_in"] = jnp.pad(p["w_in"].astype(BF16), ((0, 0), (0, 0), (0, DT_PAD - SSD_HEADS)))
    out["w_ckt"] = jnp.swapaxes(out["w_ckv"][:, :, :D_MODEL], 1, 2)
    out["w_vt"] = jnp.swapaxes(out["w_in"][:, :, _C_VB:_C_Z], 1, 2)
    out["w_dtt"] = jnp.swapaxes(out["w_in"][:, :, _C_DT:D_IN_PROJ], 1, 2)
    pad_row = lambda v: jnp.pad(v.astype(F32), ((0, 0), (0, DT_PAD - SSD_HEADS))).reshape(depth, 1, DT_PAD)
    col = lambda v: v.astype(F32).reshape(depth, SSD_HEADS, 1)
    out.update(conv_w=p["conv_w"].astype(F32), dt_bias_row=pad_row(p["dt_bias"]), dt_bias_col=col(p["dt_bias"]),
               a_log_row=pad_row(p["a_log"]), a_log_col=col(p["a_log"]),
               d_skip=jnp.repeat(p["d_skip"].astype(F32), SSD_HEAD_DIM, axis=1).reshape(depth, 1, SSD_D_INNER),
               swa_sinks=p["swa_sinks"].astype(F32))
    return out


def _ssd_constants():
    head_of_col = jnp.arange(SSD_D_INNER) // SSD_HEAD_DIM
    expand = (jnp.arange(DT_PAD)[:, None] == head_of_col[None, :]).astype(BF16)
    t_idx = jnp.arange(SSD_CHUNK)[None, :, None]
    j_idx = jnp.arange(SSD_CONV - 1)[:, None, None]
    col = jnp.arange(SSD_CHUNK + CONV_TAIL)[None, None, :]
    shift = (col == CONV_TAIL + t_idx - (SSD_CONV - 1) + j_idx).astype(BF16)
    return expand, shift.reshape((SSD_CONV - 1) * SSD_CHUNK, SSD_CHUNK + CONV_TAIL)


def _layer_params(l, sp, expand, shift):
    lp = {k: _LayerParam(v, l) for k, v in sp.items() if k != "swa_sinks"}
    lp["swa_sinks"] = sp["swa_sinks"][l]
    lp["ssd"] = {"conv_w": lp["conv_w"], "conv_b": lp["conv_b"], "dt_bias_row": lp["dt_bias_row"],
                 "dt_bias_col": lp["dt_bias_col"], "a_log_row": lp["a_log_row"], "a_log_col": lp["a_log_col"],
                 "d_skip": lp["d_skip"], "norm_w": lp["ssd_norm"], "expand": expand, "shift": shift}
    return lp


def kernel(x, mem, positions, norm_mix, w_in, w_gate, b_gate, swa_sinks, conv_w, conv_b, dt_bias, a_log, d_skip,
           ssd_norm, w_br_swa, w_br_moba, w_br_ssd, w_mix_out, norm_cross, norm_mem, w_cq, w_ckv, w_co, norm_mlp,
           w_up, w_down, final_norm):
    batch, seq, d = x.shape
    depth = w_in.shape[0]
    assert d == D_MODEL and seq % MOBA_BLOCK == 0 and mem.shape[1] == MEM_LEN
    params = dict(norm_mix=norm_mix, w_in=w_in, w_gate=w_gate, b_gate=b_gate, swa_sinks=swa_sinks, conv_w=conv_w,
                  conv_b=conv_b, dt_bias=dt_bias, a_log=a_log, d_skip=d_skip, ssd_norm=ssd_norm, w_br_swa=w_br_swa,
                  w_br_moba=w_br_moba, w_br_ssd=w_br_ssd, w_mix_out=w_mix_out, norm_cross=norm_cross,
                  norm_mem=norm_mem, w_cq=w_cq, w_ckv=w_ckv, w_co=w_co, norm_mlp=norm_mlp, w_up=w_up, w_down=w_down)
    x2d = x.reshape(batch * seq, d)
    mem2d = mem.reshape(batch * MEM_LEN, d)
    fnw = final_norm.reshape(1, d).astype(F32)
    cos_t, sin_t = _rope_tables(positions)
    stacked = _prep_params(params)
    expand, shift = _ssd_constants()
    for l in range(depth):
        lp = _layer_params(l, stacked, expand, shift)
        (q_a, k_a, v_a, q_b, k_b, kmean, vt_b, z, xbc, dt, dtt) = _proj_in(
            x2d, lp["norm_mix"], lp["w_in"], lp["w_vt"], lp["w_dtt"], cos_t, sin_t, batch)
        y_a = _swa(q_a, k_a, v_a, lp["swa_sinks"], batch)
        y_b = _moba(q_b, k_b, vt_b, kmean, batch)
        y_c = _ssd(z, xbc, dt, dtt, lp["ssd"], batch)
        ktmem, vmem = _mem_kv(mem2d, lp["norm_mem"], lp["w_ckt"], lp["w_ckv"])
        x2d = _merge(x2d, y_a, y_b, y_c, ktmem, vmem, lp, fnw, batch, final=(l == depth - 1))
    return x2d.reshape(batch, seq, d)
```

```python
import functools
import math
from typing import NamedTuple

import jax
import jax.numpy as jnp
from jax import lax
from jax.experimental import pallas as pl
from jax.experimental.pallas import tpu as pltpu

F32 = jnp.float32
BF16 = jnp.bfloat16

D_MODEL = 1024
MEM_LEN = 256
NORM_EPS = 1e-6
SWA_Q_HEADS = 8
SWA_KV_HEADS = 2
SWA_HEAD_DIM = 64
SWA_BLOCK = 128
SWA_BLOCKS_PER_STEP = 8
ROPE_THETA = 150000.0
MOBA_HEADS = 8
MOBA_HEAD_DIM = 64
MOBA_BLOCK = 256
MOBA_TOPK = 3
MOBA_GROUP = 4
MOBA_SCORE_SPAN = 1
MOBA_PAIRS_PER_STEP = 4
SUM_ROWS = 16
MOBA_VT_ROWS = MOBA_HEADS * (MOBA_HEAD_DIM + SUM_ROWS)
SSD_D_INNER = D_MODEL
SSD_HEAD_DIM = 64
SSD_HEADS = SSD_D_INNER // SSD_HEAD_DIM
SSD_GROUPS = 2
SSD_D_STATE = 128
SSD_CONV = 4
SSD_CHUNK = 128
SSD_CHUNKS_PER_STEP = 8
X_HEADS = 4
X_HEAD_DIM = D_MODEL // X_HEADS
D_FF = 4 * D_MODEL
N_BRANCH = 3

SWA_Q_W = SWA_Q_HEADS * SWA_HEAD_DIM
SWA_KV_W = SWA_KV_HEADS * SWA_HEAD_DIM
MOBA_W = MOBA_HEADS * MOBA_HEAD_DIM
SSD_BC_W = SSD_GROUPS * SSD_D_STATE
SSD_XBC_W = SSD_D_INNER + 2 * SSD_BC_W
SSD_GROUP_W = SSD_D_INNER // SSD_GROUPS

LANES = 128
HEAD_PAIR_W = 2 * MOBA_HEAD_DIM
DT_PAD = LANES
CONV_TAIL = 16
ROW_TILE = 256
MLP_ROW_TILE = 512
VMEM_LIMIT = 56 * 1024 * 1024

_C_QA = 0
_C_KA = _C_QA + SWA_Q_W
_C_VA = _C_KA + SWA_KV_W
_C_QB = _C_VA + SWA_KV_W
_C_KB = _C_QB + MOBA_W
_C_VB = _C_KB + MOBA_W
_C_Z = _C_VB + MOBA_W
_C_XBC = _C_Z + SSD_D_INNER
_C_DT = _C_XBC + SSD_XBC_W
D_IN_PROJ = _C_DT + SSD_HEADS

LOG2E = math.log2(math.e)
_NT = (((1,), (1,)), ((), ()))


class _LayerParam(NamedTuple):
    stacked: jax.Array
    layer: int


def _const_spec(a):
    if isinstance(a, _LayerParam):
        shape = a.stacked.shape[1:]
        layer = a.layer
        return pl.BlockSpec((None,) + shape, lambda *_: (layer,) + (0,) * len(shape), pipeline_mode=pl.Buffered(1))
    return pl.BlockSpec(a.shape, lambda *_: (0,) * a.ndim, pipeline_mode=pl.Buffered(1))


def _operands(*args):
    return tuple(a.stacked if isinstance(a, _LayerParam) else a for a in args)


def _cparams(n_axes, flags=None):
    return pltpu.CompilerParams(dimension_semantics=("arbitrary",) * n_axes,
                                vmem_limit_bytes=VMEM_LIMIT, flags=flags)


def _rms(x, w):
    ms = jnp.mean(x * x, axis=-1, keepdims=True)
    return x * lax.rsqrt(ms + NORM_EPS) * w


def _dot(a, b):
    return jnp.dot(a, b, preferred_element_type=F32)


def _dot_nt(a, b, precision=None):
    return lax.dot_general(a, b, _NT, preferred_element_type=F32, precision=precision)


def _silu(x):
    half = 0.5 * x
    return half + half * jnp.tanh(half)


def _rope_table_kernel(pos_ref, inv_ref, cos_ref, sin_ref):
    ang = pos_ref[...].astype(F32) * inv_ref[...]
    lane = lax.broadcasted_iota(jnp.int32, ang.shape, 1)
    first_half = (lane % SWA_HEAD_DIM) < (SWA_HEAD_DIM // 2)
    cos_ref[...] = jnp.cos(ang)
    sin_ref[...] = jnp.where(first_half, -jnp.sin(ang), jnp.sin(ang))


def _rope_tables(positions):
    m = positions.size
    half = SWA_HEAD_DIM // 2
    inv = ROPE_THETA ** (-jnp.arange(half, dtype=F32) / half)
    inv = jnp.tile(inv, LANES // half).reshape(1, LANES)
    t = 1024 if m % 1024 == 0 else ROW_TILE
    return pl.pallas_call(
        _rope_table_kernel,
        out_shape=(jax.ShapeDtypeStruct((m, LANES), F32), jax.ShapeDtypeStruct((m, LANES), F32)),
        grid=(m // t,),
        in_specs=[pl.BlockSpec((t, 1), lambda i: (i, 0)), pl.BlockSpec((1, LANES), lambda i: (0, 0))],
        out_specs=(pl.BlockSpec((t, LANES), lambda i: (i, 0)), pl.BlockSpec((t, LANES), lambda i: (i, 0))),
        compiler_params=_cparams(1),
        name="rope_tables",
    )(positions.reshape(m, 1), inv)


def _rope(t, cos, sin):
    w = t.shape[-1]
    reps = w // LANES
    cos_w = jnp.concatenate([cos] * reps, axis=1)
    sin_w = jnp.concatenate([sin] * reps, axis=1)
    half = SWA_HEAD_DIM // 2
    lane = lax.broadcasted_iota(jnp.int32, t.shape, 1)
    first_half = (lane % SWA_HEAD_DIM) < half
    partner = jnp.where(first_half, pltpu.roll(t, w - half, 1), pltpu.roll(t, half, 1))
    return t * cos_w + partner * sin_w


def _dup_kv_heads(t):
    swapped = pltpu.roll(t, SWA_HEAD_DIM, 1)
    low = lax.broadcasted_iota(jnp.int32, t.shape, 1) < SWA_HEAD_DIM
    return jnp.concatenate([jnp.where(low, t, swapped), jnp.where(low, swapped, t)], axis=1)


def _proj_in_kernel(nkb, x_ref, nw_ref, w_ref, wvt_ref, wdt_ref, wdtt_ref, cos_ref, sin_ref,
                    qa_ref, ka_ref, va_ref, qb_ref, kb_ref, kmean_ref, vt_ref,
                    z_ref, xbc_ref, dt_ref, dtt_ref):
    i = pl.program_id(0)
    hf = _rms(x_ref[...], nw_ref[...])
    h = hf.astype(BF16)
    ht = hf.T.astype(BF16)
    cos = cos_ref[...]
    sin = sin_ref[...]

    def seg(a, b):
        return _dot(h, w_ref[:, a:b])

    qa_ref[...] = _rope(seg(_C_QA, _C_KA), cos, sin).astype(BF16)
    ka_ref[...] = _dup_kv_heads(_rope(seg(_C_KA, _C_VA), cos, sin)).astype(BF16)
    va_ref[...] = _dup_kv_heads(seg(_C_VA, _C_QB)).astype(BF16)
    qb_ref[...] = seg(_C_QB, _C_KB).astype(BF16)
    kb = seg(_C_KB, _C_VB)
    kb_ref[...] = (kb * LOG2E).astype(BF16)
    per_tile = kb.shape[0] // MOBA_BLOCK
    for j in range(per_tile):
        blk_mean = jnp.mean(kb[j * MOBA_BLOCK:(j + 1) * MOBA_BLOCK], axis=0, keepdims=True)
        kmean_ref[0, pl.ds((i * per_tile + j) % nkb, 1), :] = blk_mean
    vt = _dot(wvt_ref[...], ht).astype(BF16)
    ext = MOBA_HEAD_DIM + SUM_ROWS
    for hh in range(MOBA_HEADS):
        vt_ref[hh * ext:hh * ext + MOBA_HEAD_DIM, :] = vt[hh * MOBA_HEAD_DIM:(hh + 1) * MOBA_HEAD_DIM]
        vt_ref[hh * ext + MOBA_HEAD_DIM:(hh + 1) * ext, :] = jnp.ones((SUM_ROWS, vt.shape[1]), BF16)
    z_ref[...] = seg(_C_Z, _C_XBC).astype(BF16)
    xbc_ref[...] = seg(_C_XBC, _C_DT).astype(BF16)
    dt_ref[...] = _dot(h, wdt_ref[...])
    dtt_ref[...] = _dot(wdtt_ref[...], ht)


def _proj_in(x2d, norm_w, w_in, w_vt, w_dt, w_dtt, cos_t, sin_t, batch):
    m = x2d.shape[0]
    t = MLP_ROW_TILE
    n = m // t
    nkb = m // batch // MOBA_BLOCK
    per_b = n // batch
    row = lambda w: pl.BlockSpec((t, w), lambda i: (i, 0))
    full = _const_spec
    out_shape = (
        jax.ShapeDtypeStruct((m, SWA_Q_W), BF16),
        jax.ShapeDtypeStruct((m, 2 * SWA_KV_W), BF16),
        jax.ShapeDtypeStruct((m, 2 * SWA_KV_W), BF16),
        jax.ShapeDtypeStruct((m, MOBA_W), BF16),
        jax.ShapeDtypeStruct((m, MOBA_W), BF16),
        jax.ShapeDtypeStruct((batch, nkb, MOBA_W), F32),
        jax.ShapeDtypeStruct((MOBA_VT_ROWS, m), BF16),
        jax.ShapeDtypeStruct((m, SSD_D_INNER), BF16),
        jax.ShapeDtypeStruct((m, SSD_XBC_W), BF16),
        jax.ShapeDtypeStruct((m, DT_PAD), F32),
        jax.ShapeDtypeStruct((SSD_HEADS, m), F32),
    )
    out_specs = (
        row(SWA_Q_W), row(2 * SWA_KV_W), row(2 * SWA_KV_W), row(MOBA_W), row(MOBA_W),
        pl.BlockSpec((1, nkb, MOBA_W), lambda i: (i // per_b, 0, 0)),
        pl.BlockSpec((MOBA_VT_ROWS, t), lambda i: (0, i)),
        row(SSD_D_INNER), row(SSD_XBC_W), row(DT_PAD),
        pl.BlockSpec((SSD_HEADS, t), lambda i: (0, i)),
    )
    return pl.pallas_call(
        functools.partial(_proj_in_kernel, nkb),
        out_shape=out_shape,
        grid=(n,),
        in_specs=[row(D_MODEL), full(norm_w), full(w_in), full(w_vt), full(w_dt), full(w_dtt),
                  row(LANES), row(LANES)],
        out_specs=out_specs,
        compiler_params=_cparams(1),
        name="proj_in",
    )(*_operands(x2d, norm_w, w_in, w_vt, w_dt, w_dtt, cos_t, sin_t))


def _swa_kernel(sink_ref, q_ref, kp_ref, kc_ref, vp_ref, vc_ref, o_ref):
    i = pl.program_id(1)
    blk = SWA_BLOCK
    kall = jnp.concatenate([kp_ref[...], kc_ref[...]], axis=0)
    vall = jnp.concatenate([vp_ref[...], vc_ref[...]], axis=0)
    qi = lax.broadcasted_iota(jnp.int32, (blk, 2 * blk), 0)
    si = lax.broadcasted_iota(jnp.int32, (blk, 2 * blk), 1)
    delta = qi + blk - si
    in_window = (delta >= 0) & (delta < blk)
    lane = lax.broadcasted_iota(jnp.int32, (blk, LANES), 1)
    low = lane < SWA_HEAD_DIM
    group = SWA_Q_HEADS // SWA_KV_HEADS
    for sub in range(SWA_BLOCKS_PER_STEP):
        rows = slice(sub * blk, (sub + 1) * blk)
        mask = in_window & ((si >= blk) | (i > 0)) if sub == 0 else in_window
        q = q_ref[rows, :] * (SWA_HEAD_DIM ** -0.5)
        kcat = kall[sub * blk:(sub + 2) * blk]
        vcat = vall[sub * blk:(sub + 2) * blk]
        outs = []
        for hd in range(SWA_Q_HEADS):
            g = hd // group
            kd = kcat[:, g * LANES:(g + 1) * LANES]
            vd = vcat[:, g * LANES:(g + 1) * LANES]
            slab = q[:, (hd // 2) * LANES:(hd // 2 + 1) * LANES]
            qm = jnp.where(low if hd % 2 == 0 else ~low, slab, jnp.zeros_like(slab))
            s = jnp.where(mask, _dot_nt(qm, kd), -jnp.inf)
            sink = sink_ref[hd]
            mx = jnp.maximum(jnp.max(s, axis=-1, keepdims=True), sink)
            p = jnp.exp(s - mx)
            denom = jnp.sum(p, axis=-1, keepdims=True) + jnp.exp(sink - mx)
            outs.append(_dot(p.astype(BF16), vd) / denom)
        for pr in range(SWA_Q_HEADS // 2):
            o_ref[rows, pr * LANES:(pr + 1) * LANES] = jnp.where(low, outs[2 * pr], outs[2 * pr + 1]).astype(BF16)


def _swa(q_a, k_a, v_a, sinks, batch):
    m = q_a.shape[0]
    blk = SWA_BLOCK
    per = SWA_BLOCKS_PER_STEP
    nb = m // batch // blk
    steps = nb // per
    kvw = 2 * SWA_KV_W
    cur = lambda b, i: (b * steps + i, 0)
    prev = lambda b, i: (b * nb + jnp.maximum(per * i - 1, 0), 0)
    return pl.pallas_call(
        _swa_kernel,
        out_shape=jax.ShapeDtypeStruct((m, SWA_Q_W), BF16),
        grid=(batch, steps),
        in_specs=[pl.BlockSpec(memory_space=pltpu.SMEM),
                  pl.BlockSpec((per * blk, SWA_Q_W), cur),
                  pl.BlockSpec((blk, kvw), prev), pl.BlockSpec((per * blk, kvw), cur),
                  pl.BlockSpec((blk, kvw), prev), pl.BlockSpec((per * blk, kvw), cur)],
        out_specs=pl.BlockSpec((per * blk, SWA_Q_W), cur),
        compiler_params=_cparams(2),
        name="swa",
    )(sinks, q_a, k_a, k_a, v_a, v_a)


def _fold8(x, op):
    return functools.reduce(op, [x[r:r + 8] for r in range(0, x.shape[0], 8)])


def _moba_kernel(q_ref, k_ref, vt_ref, kmean_ref, o_ref, *scratch):
    i = pl.program_id(2)
    per_pipe = len(scratch) // MOBA_PAIRS_PER_STEP
    pipes = []
    for ps in range(MOBA_PAIRS_PER_STEP):
        lanes = pl.ds(ps * HEAD_PAIR_W, HEAD_PAIR_W)
        vt_rows = pl.ds(ps * 2 * (MOBA_HEAD_DIM + SUM_ROWS), 2 * (MOBA_HEAD_DIM + SUM_ROWS))
        pipes.append(_moba_pipeline(i, q_ref.at[:, lanes], k_ref.at[:, lanes], vt_ref.at[vt_rows, :],
                                    kmean_ref.at[:, :, lanes], o_ref.at[:, lanes],
                                    *scratch[ps * per_pipe:(ps + 1) * per_pipe]))
    n_groups = (i + MOBA_GROUP - 1) // MOBA_GROUP

    for p in pipes:
        p["setup"]()
    for p in pipes:
        p["scores_own"]()
    mx0 = [None] * len(pipes)
    for r in range(0, MOBA_GROUP, MOBA_SCORE_SPAN):
        for n, p in enumerate(pipes):
            mx0[n] = p["score_blocks"](0, 0, r, mx0[n])
            if r == 0:
                p["accumulate_own"]()
    for n, p in enumerate(pipes):
        p["end_scores"](0, mx0[n])

    def stage(score_g, score_buf, acc_g, acc_buf):
        state = [p["begin"](acc_buf) for p in pipes]
        mx = [None] * len(pipes)
        for r in range(0, MOBA_GROUP, MOBA_SCORE_SPAN):
            for n, p in enumerate(pipes):
                mx[n] = p["score_blocks"](score_g, score_buf, r, mx[n])
                for rr in range(r, r + MOBA_SCORE_SPAN):
                    p["exp_block"](acc_g, acc_buf, rr, state[n])
        for n, p in enumerate(pipes):
            p["end_scores"](score_buf, mx[n])

    def pair(u, carry):
        g = 2 * u
        stage(g + 1, 1, g, 0)
        stage(g + 2, 0, g + 1, 1)
        return carry

    lax.fori_loop(0, n_groups // 2, pair, 0)

    @pl.when(n_groups % 2 == 1)
    def _():
        for p in pipes:
            p["accumulate"](n_groups - 1, 0)

    for p in pipes:
        p["finalize"]()


def _moba_pipeline(i, q_ref, k_ref, vt_ref, kmean_ref, o_ref, qm_ref, sel_ref, ml_ref, acc_ref,
                   so_ref, mxo_ref, po_ref, sa_ref, mxa_ref, pa_ref, sb_ref, mxb_ref, pb_ref):
    blk = MOBA_BLOCK
    hd = MOBA_HEAD_DIM
    grp = MOBA_GROUP
    nkb = kmean_ref.shape[1]
    stage_bufs = ((sa_ref, mxa_ref, pa_ref), (sb_ref, mxb_ref, pb_ref))

    def setup():
        qt = (q_ref[...].astype(F32) * (hd ** -0.5)).T
        row = lax.broadcasted_iota(jnp.int32, qt.shape, 0)
        qm_ref[:, 0:blk] = jnp.where(row < hd, qt, 0.0).astype(BF16)
        qm_ref[:, blk:2 * blk] = jnp.where(row >= hd, qt, 0.0).astype(BF16)

        blk_id = lax.broadcasted_iota(jnp.int32, (nkb, 2 * blk), 0)
        gate = None
        resid = kmean_ref[0]
        for _ in range(3):
            part = resid.astype(BF16)
            resid = resid - part.astype(F32)
            term = _dot(part, qm_ref[...])
            gate = term if gate is None else gate + term
        gate = jnp.where(blk_id < i, gate, -jnp.inf)
        sel = jnp.zeros((nkb, 2 * blk), F32)
        for _ in range(min(MOBA_TOPK, nkb)):
            best = jnp.max(gate, axis=0, keepdims=True)
            idx = jnp.min(jnp.where(gate == best, blk_id, nkb), axis=0, keepdims=True)
            pick = blk_id == idx
            sel = jnp.where(pick & (blk_id < i), 1.0, sel)
            gate = jnp.where(pick, -jnp.inf, gate)
        sel_ref[0:nkb, :] = sel
        sel_ref[nkb:nkb + grp, :] = jnp.zeros((grp, 2 * blk), F32)

    def score_blocks(start, r, keeps, s_buf, mx):
        n = len(keeps)
        s_all = _dot(k_ref[pl.ds(start + r * blk, n * blk), :], qm_ref[...])
        for j, keep in enumerate(keeps):
            s = jnp.where(keep, s_all[j * blk:(j + 1) * blk], -jnp.inf)
            s_buf[(r + j) * blk:(r + j + 1) * blk, :] = s
            m8 = _fold8(s, jnp.maximum)
            mx = m8 if mx is None else jnp.maximum(mx, m8)
        return mx

    def begin(mx_buf, first):
        mx = jnp.max(mx_buf[...], axis=0, keepdims=True)
        if first:
            m_new, alpha = mx, None
        else:
            m_old = ml_ref[0:1, :]
            m_new = jnp.maximum(m_old, mx)
            alpha = jnp.exp2(m_old - m_new)
            for h in range(2):
                acc_ref[h] = alpha[:, h * blk:(h + 1) * blk] * acc_ref[h]
        ml_ref[0:1, :] = m_new
        return m_new, alpha

    def exp_block(s_buf, p_buf, r, m_new):
        p_buf[r * blk:(r + 1) * blk, :] = jnp.exp2(s_buf[r * blk:(r + 1) * blk, :] - m_new).astype(BF16)

    def pv_block(start, r, p_buf, assign):
        ext = hd + SUM_ROWS
        for h in range(2):
            vt_ext = vt_ref[h * ext:(h + 1) * ext, pl.ds(start + r * blk, blk)]
            term = _dot(vt_ext, p_buf[r * blk:(r + 1) * blk, h * blk:(h + 1) * blk])
            acc_ref[h] = term if assign else acc_ref[h] + term

    def group_start(g):
        return pl.multiple_of(jnp.minimum(g, nkb // grp - 1) * (grp * blk), grp * blk)

    def group_score_blocks(g, buf, r, mx):
        keeps = [sel_ref[pl.ds(g * grp + r + j, 1), :] > 0.0 for j in range(MOBA_SCORE_SPAN)]
        return score_blocks(group_start(g), r, keeps, stage_bufs[buf][0], mx)

    def group_end_scores(buf, mx):
        stage_bufs[buf][1][...] = mx

    def group_begin(buf):
        return begin(stage_bufs[buf][1], False)

    def group_exp_block(g, buf, r, state):
        exp_block(stage_bufs[buf][0], stage_bufs[buf][2], r, state[0])
        pv_block(group_start(g), r, stage_bufs[buf][2], False)

    def group_accumulate(g, buf):
        state = group_begin(buf)
        for r in range(grp):
            group_exp_block(g, buf, r, state)

    own = pl.multiple_of(i * blk, blk)

    def scores_own():
        kpos = lax.broadcasted_iota(jnp.int32, (blk, 2 * blk), 0)
        qpos = lax.broadcasted_iota(jnp.int32, (blk, 2 * blk), 1) % blk
        mxo_ref[...] = score_blocks(own, 0, [kpos <= qpos], so_ref, None)

    def accumulate_own():
        m_new, _ = begin(mxo_ref, True)
        exp_block(so_ref, po_ref, 0, m_new)
        pv_block(own, 0, po_ref, True)

    def finalize():
        out_t = jnp.concatenate([acc_ref[h, 0:hd, :] / acc_ref[h, hd:hd + 1, :] for h in range(2)], axis=0)
        o_ref[...] = out_t.T.astype(BF16)

    return dict(setup=setup, scores_own=scores_own, accumulate_own=accumulate_own,
                accumulate=group_accumulate, finalize=finalize, begin=group_begin,
                score_blocks=group_score_blocks, exp_block=group_exp_block, end_scores=group_end_scores)


def _moba_stage(n_blocks, blk):
    return [pltpu.VMEM((n_blocks * blk, 2 * blk), F32), pltpu.VMEM((8, 2 * blk), F32),
            pltpu.VMEM((n_blocks * blk, 2 * blk), BF16)]


def _moba(q_b, k_b, vt_b, kmean, batch):
    m = q_b.shape[0]
    s = m // batch
    blk = MOBA_BLOCK
    nkb = s // blk
    assert nkb % MOBA_GROUP == 0
    w = MOBA_PAIRS_PER_STEP * HEAD_PAIR_W
    steps = MOBA_W // w
    pipe_scratch = [pltpu.VMEM((HEAD_PAIR_W, 2 * blk), BF16),
                    pltpu.VMEM((nkb + MOBA_GROUP, 2 * blk), F32),
                    pltpu.VMEM((8, 2 * blk), F32),
                    pltpu.VMEM((2, MOBA_HEAD_DIM + SUM_ROWS, blk), F32),
                    ] + _moba_stage(1, blk) + _moba_stage(MOBA_GROUP, blk) + _moba_stage(MOBA_GROUP, blk)
    return pl.pallas_call(
        _moba_kernel,
        out_shape=jax.ShapeDtypeStruct((m, MOBA_W), BF16),
        grid=(batch, steps, nkb),
        in_specs=[pl.BlockSpec((blk, w), lambda b, p, i: (b * nkb + i, p)),
                  pl.BlockSpec((s, w), lambda b, p, i: (b, p), pipeline_mode=pl.Buffered(1)),
                  pl.BlockSpec((MOBA_VT_ROWS // steps, s), lambda b, p, i: (p, b), pipeline_mode=pl.Buffered(1)),
                  pl.BlockSpec((1, nkb, w), lambda b, p, i: (b, 0, p))],
        out_specs=pl.BlockSpec((blk, w), lambda b, p, i: (b * nkb + i, p)),
        scratch_shapes=pipe_scratch * MOBA_PAIRS_PER_STEP,
        compiler_params=_cparams(3),
        name="moba",
    )(q_b, k_b, vt_b, kmean)


def _split_dot(v, e):
    hi = v.astype(BF16)
    lo = (v - hi.astype(F32)).astype(BF16)
    return _dot(hi, e) + _dot(lo, e)


def _ssd_kernel(z_ref, xbc_ref, dt_ref, dtt_ref, cw_ref, cb_ref, dtb_ref, dtbc_ref, alog_ref, alogc_ref,
                dskip_ref, nw_ref, exp_ref, shift_ref, o_ref, ext_ref, state_ref):
    c = pl.program_id(1)
    L = SSD_CHUNK
    tail = CONV_TAIL

    last = SSD_CHUNKS_PER_STEP * L

    @pl.when(c > 0)
    def _():
        ext_ref[0:tail, :] = ext_ref[last:last + tail, :]

    @pl.when(c == 0)
    def _():
        ext_ref[0:tail, :] = jnp.zeros((tail, SSD_XBC_W), BF16)
        state_ref[...] = jnp.zeros_like(state_ref)

    ext_ref[tail:tail + last, :] = xbc_ref[...]
    for sub in range(SSD_CHUNKS_PER_STEP):
        rows = slice(sub * L, (sub + 1) * L)
        window = ext_ref[sub * L:(sub + 1) * L + tail, :]
        _ssd_chunk(window, xbc_ref[rows, :], z_ref[rows, :], dt_ref[rows, :], dtt_ref[:, rows], cw_ref, cb_ref,
                   dtb_ref, dtbc_ref, alog_ref, alogc_ref, dskip_ref, nw_ref, exp_ref, shift_ref,
                   o_ref.at[rows, :], state_ref)


def _ssd_chunk(window, xbc, z, dt_raw, dtt_raw, cw_ref, cb_ref, dtb_ref, dtbc_ref, alog_ref, alogc_ref,
               dskip_ref, nw_ref, exp_ref, shift_ref, o_ref, state_ref):
    L = SSD_CHUNK
    shifted = _dot(shift_ref[...], window)
    acc = cb_ref[...] + cw_ref[SSD_CONV - 1:SSD_CONV, :] * xbc.astype(F32)
    for j in range(SSD_CONV - 1):
        acc = acc + cw_ref[j:j + 1, :] * shifted[j * L:(j + 1) * L, :]
    xc = _silu(acc)
    xs = xc[:, :SSD_D_INNER]
    bm = xc[:, SSD_D_INNER:SSD_D_INNER + SSD_BC_W]
    cm = xc[:, SSD_D_INNER + SSD_BC_W:]

    dt = jax.nn.softplus(dt_raw + dtb_ref[...])
    dtt = jax.nn.softplus(dtt_raw + dtbc_ref[...])
    a_row = -jnp.exp(alog_ref[...]) * LOG2E
    a_col = -jnp.exp(alogc_ref[...]) * LOG2E
    r_i = lax.broadcasted_iota(jnp.int32, (L, L), 0)
    c_i = lax.broadcasted_iota(jnp.int32, (L, L), 1)
    tril = r_i >= c_i
    hp = lax.Precision.HIGHEST
    a_cs = jnp.dot(tril.astype(F32), dt * a_row, precision=hp, preferred_element_type=F32)
    a_cst = jnp.dot(dtt * a_col, (r_i <= c_i).astype(F32), precision=hp, preferred_element_type=F32)
    a_src = a_cst - jnp.log2(dtt)
    a_last = a_cs[L - 1:L, :]
    expand = exp_ref[...]
    decay_out = _split_dot(jnp.exp2(a_cs), expand)
    decay_in = _split_dot(dt * jnp.exp2(a_last - a_cs), expand)
    chunk_decay = decay_out[L - 1:L, :]

    lane = lax.broadcasted_iota(jnp.int32, (L, LANES), 1)
    low = lane < SSD_HEAD_DIM
    heads_per_group = SSD_HEADS // SSD_GROUPS
    ys = []
    for g in range(SSD_GROUPS):
        b_g = bm[:, g * SSD_D_STATE:(g + 1) * SSD_D_STATE]
        c_g = cm[:, g * SSD_D_STATE:(g + 1) * SSD_D_STATE]
        cb = _dot_nt(c_g.astype(BF16), b_g.astype(BF16))
        gcols = slice(g * SSD_GROUP_W, (g + 1) * SSD_GROUP_W)
        y_off = _dot(c_g.astype(BF16), state_ref[g].astype(BF16)) * decay_out[:, gcols]
        y_diag = []
        for pr in range(heads_per_group // 2):
            col0 = g * SSD_GROUP_W + pr * LANES
            x_pair = xs[:, col0:col0 + LANES].astype(BF16)
            halves = []
            for e in range(2):
                hidx = g * heads_per_group + 2 * pr + e
                diff = a_cs[:, hidx:hidx + 1] - a_src[hidx:hidx + 1, :]
                wmat = cb * jnp.exp2(jnp.where(tril, diff, -jnp.inf))
                halves.append(_dot(wmat.astype(BF16), x_pair))
            y_diag.append(jnp.where(low, halves[0], halves[1]))
        ys.append(jnp.concatenate(y_diag, axis=1) + y_off)
        xw = (xs[:, gcols] * decay_in[:, gcols]).astype(BF16)
        state_ref[g] = state_ref[g] * chunk_decay[:, gcols] + _dot(b_g.T.astype(BF16), xw)

    y = jnp.concatenate(ys, axis=1) + xs * dskip_ref[...]
    y = y * _silu(z.astype(F32))
    outs = []
    for g in range(SSD_GROUPS):
        yg = y[:, g * SSD_GROUP_W:(g + 1) * SSD_GROUP_W]
        outs.append(yg * lax.rsqrt(jnp.mean(yg * yg, axis=-1, keepdims=True) + NORM_EPS))
    o_ref[...] = (jnp.concatenate(outs, axis=1) * nw_ref[...]).astype(BF16)


def _ssd(z, xbc, dt, dtt, sp, batch):
    m = z.shape[0]
    L = SSD_CHUNKS_PER_STEP * SSD_CHUNK
    nc = m // batch // L
    row = lambda w: pl.BlockSpec((L, w), lambda b, c: (b * nc + c, 0))
    full = _const_spec
    consts = (sp["conv_w"], sp["conv_b"], sp["dt_bias_row"], sp["dt_bias_col"], sp["a_log_row"],
              sp["a_log_col"], sp["d_skip"], sp["norm_w"], sp["expand"], sp["shift"])
    return pl.pallas_call(
        _ssd_kernel,
        out_shape=jax.ShapeDtypeStruct((m, SSD_D_INNER), BF16),
        grid=(batch, nc),
        in_specs=[row(SSD_D_INNER), row(SSD_XBC_W), row(DT_PAD),
                  pl.BlockSpec((SSD_HEADS, L), lambda b, c: (0, b * nc + c))] + [full(a) for a in consts],
        out_specs=row(SSD_D_INNER),
        scratch_shapes=[pltpu.VMEM((L + CONV_TAIL, SSD_XBC_W), BF16),
                        pltpu.VMEM((SSD_GROUPS, SSD_D_STATE, SSD_GROUP_W), F32)],
        compiler_params=_cparams(2),
        name="ssd",
    )(*_operands(z, xbc, dt, dtt, *consts))


def _mem_kv_kernel(mem_ref, nw_ref, wkt_ref, w_ref, kt_ref, v_ref):
    hm = _rms(mem_ref[...], nw_ref[...])
    kt_ref[...] = _dot(wkt_ref[...], hm.T.astype(BF16)).astype(BF16)
    v_ref[...] = _dot(hm.astype(BF16), w_ref[:, D_MODEL:]).astype(BF16)


def _mem_kv(mem2d, norm_w, w_ckt, w_ckv):
    m = mem2d.shape[0]
    t = MEM_LEN
    row = pl.BlockSpec((t, D_MODEL), lambda i: (i, 0))
    full = _const_spec
    return pl.pallas_call(
        _mem_kv_kernel,
        out_shape=(jax.ShapeDtypeStruct((m // t * D_MODEL, t), BF16), jax.ShapeDtypeStruct((m, D_MODEL), BF16)),
        grid=(m // t,),
        in_specs=[row, full(norm_w), full(w_ckt), full(w_ckv)],
        out_specs=(pl.BlockSpec((D_MODEL, t), lambda i: (i, 0)), row),
        compiler_params=_cparams(1),
        name="mem_kv",
    )(*_operands(mem2d, norm_w, w_ckt, w_ckv))


def _merge_kernel(final, x_ref, ya_ref, yb_ref, yc_ref, nmix_ref, wg_ref, bg_ref, wa_ref, wb_ref, wc_ref, wmix_ref,
                  ncross_ref, wcq_ref, ktmem_ref, vmem_ref, wco_ref, nmlp_ref, wup_ref, wdown_ref, fnw_ref, o_ref):
    x = x_ref[...]
    h = _rms(x, nmix_ref[...]).astype(BF16)
    merged = None
    for br, (y_ref, w_ref) in enumerate(((ya_ref, wa_ref), (yb_ref, wb_ref), (yc_ref, wc_ref))):
        cols = slice(br * D_MODEL, (br + 1) * D_MODEL)
        gate = jax.nn.sigmoid(_dot(h, wg_ref[:, cols]) + bg_ref[:, cols])
        term = gate * _dot(y_ref[...], w_ref[...])
        merged = term if merged is None else merged + term
    x = x + _dot(merged.astype(BF16), wmix_ref[...])

    hq = _rms(x, ncross_ref[...]).astype(BF16)
    q = _dot(hq, wcq_ref[...]).astype(BF16)
    scale = X_HEAD_DIM ** -0.5
    outs = []
    for hh in range(X_HEADS):
        cols = slice(hh * X_HEAD_DIM, (hh + 1) * X_HEAD_DIM)
        s = _dot(q[:, cols], ktmem_ref[cols, :]) * scale
        p = jnp.exp(s - jnp.max(s, axis=-1, keepdims=True))
        denom = jnp.sum(p, axis=-1, keepdims=True)
        outs.append(_dot(p.astype(BF16), vmem_ref[:, cols]) / denom)
    o = jnp.concatenate(outs, axis=1).astype(BF16)
    x = x + _dot(o, wco_ref[...])

    hf = _rms(x, nmlp_ref[...]).astype(BF16)
    chunk = D_MODEL
    for c in range(D_FF // chunk):
        cols = slice(c * chunk, (c + 1) * chunk)
        u = jnp.maximum(_dot(hf, wup_ref[:, cols]), 0.0)
        x = x + _dot((u * u).astype(BF16), wdown_ref[cols, :])
    if final:
        x = _rms(x, fnw_ref[...])
    o_ref[...] = x


def _merge(x2d, y_a, y_b, y_c, ktmem, vmem, lp, fnw, batch, final):
    m = x2d.shape[0]
    t = MLP_ROW_TILE
    n = m // t
    per_b = n // batch
    row = lambda w: pl.BlockSpec((t, w), lambda i: (i, 0))
    full = _const_spec
    memspec = pl.BlockSpec((MEM_LEN, D_MODEL), lambda i: (i // per_b, 0))
    ktspec = pl.BlockSpec((D_MODEL, MEM_LEN), lambda i: (i // per_b, 0))
    consts_a = (lp["norm_mix"], lp["w_gate"], lp["b_gate"], lp["w_br_swa"], lp["w_br_moba"], lp["w_br_ssd"],
                lp["w_mix_out"], lp["norm_cross"], lp["w_cq"])
    consts_b = (lp["w_co"], lp["norm_mlp"], lp["w_up"], lp["w_down"], fnw)
    return pl.pallas_call(
        functools.partial(_merge_kernel, final),
        out_shape=jax.ShapeDtypeStruct((m, D_MODEL), F32),
        grid=(n,),
        in_specs=[row(D_MODEL), row(SWA_Q_W), row(MOBA_W), row(SSD_D_INNER)] + [full(a) for a in consts_a]
                 + [ktspec, memspec] + [full(a) for a in consts_b],
        out_specs=row(D_MODEL),
        compiler_params=_cparams(1),
        name="merge_mlp_final" if final else "merge_mlp",
    )(*_operands(x2d, y_a, y_b, y_c, *consts_a, ktmem, vmem, *consts_b))


_MATMUL_WEIGHTS = ("w_gate", "w_br_swa", "w_br_moba", "w_br_ssd", "w_mix_out", "w_cq", "w_ckv", "w_co",
                   "w_up", "w_down")
_ROW_PARAMS = ("norm_mix", "b_gate", "norm_cross", "norm_mem", "norm_mlp", "conv_b", "ssd_norm")


def _prep_params(p):
    depth = p["w_in"].shape[0]
    out = {k: p[k].astype(BF16) for k in _MATMUL_WEIGHTS}
    out.update({k: p[k].astype(F32).reshape(depth, 1, -1) for k in _ROW_PARAMS})
    out["w_in"] = p["w_in"][:, :, :_C_DT].astype(BF16)
    w_dt = p["w_in"][:, :, _C_DT:].astype(BF16)
    out["w_dt"] = jnp.pad(w_dt, ((0, 0), (0, 0), (0, DT_PAD - SSD_HEADS)))
    out["w_ckt"] = jnp.swapaxes(out["w_ckv"][:, :, :D_MODEL], 1, 2)
    out["w_vt"] = jnp.swapaxes(out["w_in"][:, :, _C_VB:_C_Z], 1, 2)
    out["w_dtt"] = jnp.swapaxes(w_dt, 1, 2)
    pad_row = lambda v: jnp.pad(v.astype(F32), ((0, 0), (0, DT_PAD - SSD_HEADS))).reshape(depth, 1, DT_PAD)
    col = lambda v: v.astype(F32).reshape(depth, SSD_HEADS, 1)
    out.update(conv_w=p["conv_w"].astype(F32), dt_bias_row=pad_row(p["dt_bias"]), dt_bias_col=col(p["dt_bias"]),
               a_log_row=pad_row(p["a_log"]), a_log_col=col(p["a_log"]),
               d_skip=jnp.repeat(p["d_skip"].astype(F32), SSD_HEAD_DIM, axis=1).reshape(depth, 1, SSD_D_INNER),
               swa_sinks=p["swa_sinks"].astype(F32))
    return out


def _ssd_constants():
    head_of_col = jnp.arange(SSD_D_INNER) // SSD_HEAD_DIM
    expand = (jnp.arange(DT_PAD)[:, None] == head_of_col[None, :]).astype(BF16)
    t_idx = jnp.arange(SSD_CHUNK)[None, :, None]
    j_idx = jnp.arange(SSD_CONV - 1)[:, None, None]
    col = jnp.arange(SSD_CHUNK + CONV_TAIL)[None, None, :]
    shift = (col == CONV_TAIL + t_idx - (SSD_CONV - 1) + j_idx).astype(BF16)
    return expand, shift.reshape((SSD_CONV - 1) * SSD_CHUNK, SSD_CHUNK + CONV_TAIL)


def _layer_params(l, sp, expand, shift):
    lp = {k: _LayerParam(v, l) for k, v in sp.items() if k != "swa_sinks"}
    lp["swa_sinks"] = sp["swa_sinks"][l]
    lp["ssd"] = {"conv_w": lp["conv_w"], "conv_b": lp["conv_b"], "dt_bias_row": lp["dt_bias_row"],
                 "dt_bias_col": lp["dt_bias_col"], "a_log_row": lp["a_log_row"], "a_log_col": lp["a_log_col"],
                 "d_skip": lp["d_skip"], "norm_w": lp["ssd_norm"], "expand": expand, "shift": shift}
    return lp


def kernel(x, mem, positions, norm_mix, w_in, w_gate, b_gate, swa_sinks, conv_w, conv_b, dt_bias, a_log, d_skip,
           ssd_norm, w_br_swa, w_br_moba, w_br_ssd, w_mix_out, norm_cross, norm_mem, w_cq, w_ckv, w_co, norm_mlp,
           w_up, w_down, final_norm):
    batch, seq, d = x.shape
    depth = w_in.shape[0]
    assert d == D_MODEL and seq % MOBA_BLOCK == 0 and mem.shape[1] == MEM_LEN
    params = dict(norm_mix=norm_mix, w_in=w_in, w_gate=w_gate, b_gate=b_gate, swa_sinks=swa_sinks, conv_w=conv_w,
                  conv_b=conv_b, dt_bias=dt_bias, a_log=a_log, d_skip=d_skip, ssd_norm=ssd_norm, w_br_swa=w_br_swa,
                  w_br_moba=w_br_moba, w_br_ssd=w_br_ssd, w_mix_out=w_mix_out, norm_cross=norm_cross,
                  norm_mem=norm_mem, w_cq=w_cq, w_ckv=w_ckv, w_co=w_co, norm_mlp=norm_mlp, w_up=w_up, w_down=w_down)
    x2d = x.reshape(batch * seq, d)
    mem2d = mem.reshape(batch * MEM_LEN, d)
    fnw = final_norm.reshape(1, d).astype(F32)
    cos_t, sin_t = _rope_tables(positions)
    stacked = _prep_params(params)
    expand, shift = _ssd_constants()
    for l in range(depth):
        lp = _layer_params(l, stacked, expand, shift)
        (q_a, k_a, v_a, q_b, k_b, kmean, vt_b, z, xbc, dt, dtt) = _proj_in(
            x2d, lp["norm_mix"], lp["w_in"], lp["w_vt"], lp["w_dt"], lp["w_dtt"], cos_t, sin_t, batch)
        y_a = _swa(q_a, k_a, v_a, lp["swa_sinks"], batch)
        y_b = _moba(q_b, k_b, vt_b, kmean, batch)
        y_c = _ssd(z, xbc, dt, dtt, lp["ssd"], batch)
        ktmem, vmem = _mem_kv(mem2d, lp["norm_mem"], lp["w_ckt"], lp["w_ckv"])
        x2d = _merge(x2d, y_a, y_b, y_c, ktmem, vmem, lp, fnw, batch, final=(l == depth - 1))
    return x2d.reshape(batch, seq, d)
```

```python
import functools
import math
from typing import NamedTuple

import jax
import jax.numpy as jnp
from jax import lax
from jax.experimental import pallas as pl
from jax.experimental.pallas import tpu as pltpu

F32 = jnp.float32
BF16 = jnp.bfloat16

D_MODEL = 1024
MEM_LEN = 256
NORM_EPS = 1e-6
SWA_Q_HEADS = 8
SWA_KV_HEADS = 2
SWA_HEAD_DIM = 64
SWA_BLOCK = 128
SWA_BLOCKS_PER_STEP = 4
ROPE_THETA = 150000.0
MOBA_HEADS = 8
MOBA_HEAD_DIM = 64
MOBA_BLOCK = 256
MOBA_TOPK = 3
MOBA_GROUP = 4
MOBA_SCORE_SPAN = 1
MOBA_PAIRS_PER_STEP = 4
SUM_ROWS = 16
MOBA_VT_ROWS = MOBA_HEADS * (MOBA_HEAD_DIM + SUM_ROWS)
SSD_D_INNER = D_MODEL
SSD_HEAD_DIM = 64
SSD_HEADS = SSD_D_INNER // SSD_HEAD_DIM
SSD_GROUPS = 2
SSD_D_STATE = 128
SSD_CONV = 4
SSD_CHUNK = 128
SSD_CHUNKS_PER_STEP = 4
X_HEADS = 4
X_HEAD_DIM = D_MODEL // X_HEADS
D_FF = 4 * D_MODEL
N_BRANCH = 3

SWA_Q_W = SWA_Q_HEADS * SWA_HEAD_DIM
SWA_KV_W = SWA_KV_HEADS * SWA_HEAD_DIM
MOBA_W = MOBA_HEADS * MOBA_HEAD_DIM
SSD_BC_W = SSD_GROUPS * SSD_D_STATE
SSD_XBC_W = SSD_D_INNER + 2 * SSD_BC_W
SSD_GROUP_W = SSD_D_INNER // SSD_GROUPS

LANES = 128
HEAD_PAIR_W = 2 * MOBA_HEAD_DIM
DT_PAD = LANES
CONV_TAIL = 16
ROW_TILE = 256
MLP_ROW_TILE = 512
VMEM_LIMIT = 56 * 1024 * 1024

_C_QA = 0
_C_KA = _C_QA + SWA_Q_W
_C_VA = _C_KA + SWA_KV_W
_C_QB = _C_VA + SWA_KV_W
_C_KB = _C_QB + MOBA_W
_C_VB = _C_KB + MOBA_W
_C_Z = _C_VB + MOBA_W
_C_XBC = _C_Z + SSD_D_INNER
_C_DT = _C_XBC + SSD_XBC_W
D_IN_PROJ = _C_DT + SSD_HEADS

LOG2E = math.log2(math.e)
_NT = (((1,), (1,)), ((), ()))


class _LayerParam(NamedTuple):
    stacked: jax.Array
    layer: int


def _const_spec(a):
    if isinstance(a, _LayerParam):
        shape = a.stacked.shape[1:]
        layer = a.layer
        return pl.BlockSpec((None,) + shape, lambda *_: (layer,) + (0,) * len(shape), pipeline_mode=pl.Buffered(1))
    return pl.BlockSpec(a.shape, lambda *_: (0,) * a.ndim, pipeline_mode=pl.Buffered(1))


def _operands(*args):
    return tuple(a.stacked if isinstance(a, _LayerParam) else a for a in args)


def _cparams(n_axes, flags=None):
    return pltpu.CompilerParams(dimension_semantics=("arbitrary",) * n_axes,
                                vmem_limit_bytes=VMEM_LIMIT, flags=flags)


def _rms(x, w):
    ms = jnp.mean(x * x, axis=-1, keepdims=True)
    return x * lax.rsqrt(ms + NORM_EPS) * w


def _dot(a, b):
    return jnp.dot(a, b, preferred_element_type=F32)


def _dot_nt(a, b, precision=None):
    return lax.dot_general(a, b, _NT, preferred_element_type=F32, precision=precision)


def _silu(x):
    half = 0.5 * x
    return half + half * jnp.tanh(half)


def _rope_table_kernel(pos_ref, inv_ref, cos_ref, sin_ref):
    ang = pos_ref[...].astype(F32) * inv_ref[...]
    lane = lax.broadcasted_iota(jnp.int32, ang.shape, 1)
    first_half = (lane % SWA_HEAD_DIM) < (SWA_HEAD_DIM // 2)
    cos_ref[...] = jnp.cos(ang)
    sin_ref[...] = jnp.where(first_half, -jnp.sin(ang), jnp.sin(ang))


def _rope_tables(positions):
    m = positions.size
    half = SWA_HEAD_DIM // 2
    inv = ROPE_THETA ** (-jnp.arange(half, dtype=F32) / half)
    inv = jnp.tile(inv, LANES // half).reshape(1, LANES)
    t = 1024 if m % 1024 == 0 else ROW_TILE
    return pl.pallas_call(
        _rope_table_kernel,
        out_shape=(jax.ShapeDtypeStruct((m, LANES), F32), jax.ShapeDtypeStruct((m, LANES), F32)),
        grid=(m // t,),
        in_specs=[pl.BlockSpec((t, 1), lambda i: (i, 0)), pl.BlockSpec((1, LANES), lambda i: (0, 0))],
        out_specs=(pl.BlockSpec((t, LANES), lambda i: (i, 0)), pl.BlockSpec((t, LANES), lambda i: (i, 0))),
        compiler_params=_cparams(1),
        name="rope_tables",
    )(positions.reshape(m, 1), inv)


def _rope(t, cos, sin):
    w = t.shape[-1]
    reps = w // LANES
    cos_w = jnp.concatenate([cos] * reps, axis=1)
    sin_w = jnp.concatenate([sin] * reps, axis=1)
    half = SWA_HEAD_DIM // 2
    lane = lax.broadcasted_iota(jnp.int32, t.shape, 1)
    first_half = (lane % SWA_HEAD_DIM) < half
    partner = jnp.where(first_half, pltpu.roll(t, w - half, 1), pltpu.roll(t, half, 1))
    return t * cos_w + partner * sin_w


def _dup_kv_heads(t):
    swapped = pltpu.roll(t, SWA_HEAD_DIM, 1)
    low = lax.broadcasted_iota(jnp.int32, t.shape, 1) < SWA_HEAD_DIM
    return jnp.concatenate([jnp.where(low, t, swapped), jnp.where(low, swapped, t)], axis=1)


def _proj_in_kernel(nkb, x_ref, nw_ref, w_ref, wvt_ref, wdtt_ref, cos_ref, sin_ref,
                    qa_ref, ka_ref, va_ref, qb_ref, kb_ref, kmean_ref, vt_ref,
                    z_ref, xbc_ref, dt_ref, dtt_ref):
    i = pl.program_id(0)
    hf = _rms(x_ref[...], nw_ref[...])
    h = hf.astype(BF16)
    ht = hf.T.astype(BF16)
    cos = cos_ref[...]
    sin = sin_ref[...]

    def seg(a, b):
        return _dot(h, w_ref[:, a:b])

    qa_ref[...] = _rope(seg(_C_QA, _C_KA), cos, sin).astype(BF16)
    ka_ref[...] = _dup_kv_heads(_rope(seg(_C_KA, _C_VA), cos, sin)).astype(BF16)
    va_ref[...] = _dup_kv_heads(seg(_C_VA, _C_QB)).astype(BF16)
    qb_ref[...] = seg(_C_QB, _C_KB).astype(BF16)
    kb = seg(_C_KB, _C_VB)
    kb_ref[...] = (kb * LOG2E).astype(BF16)
    per_tile = kb.shape[0] // MOBA_BLOCK
    for j in range(per_tile):
        blk_mean = jnp.mean(kb[j * MOBA_BLOCK:(j + 1) * MOBA_BLOCK], axis=0, keepdims=True)
        kmean_ref[0, pl.ds((i * per_tile + j) % nkb, 1), :] = blk_mean
    vt = _dot(wvt_ref[...], ht).astype(BF16)
    ext = MOBA_HEAD_DIM + SUM_ROWS
    for hh in range(MOBA_HEADS):
        vt_ref[hh * ext:hh * ext + MOBA_HEAD_DIM, :] = vt[hh * MOBA_HEAD_DIM:(hh + 1) * MOBA_HEAD_DIM]
        vt_ref[hh * ext + MOBA_HEAD_DIM:(hh + 1) * ext, :] = jnp.ones((SUM_ROWS, vt.shape[1]), BF16)
    z_ref[...] = seg(_C_Z, _C_XBC).astype(BF16)
    xbc_ref[...] = seg(_C_XBC, _C_DT).astype(BF16)
    dt_ref[...] = seg(_C_DT, _C_DT + DT_PAD)
    dtt_ref[...] = _dot(wdtt_ref[...], ht)


def _proj_in(x2d, norm_w, w_in, w_vt, w_dtt, cos_t, sin_t, batch):
    m = x2d.shape[0]
    t = MLP_ROW_TILE
    n = m // t
    nkb = m // batch // MOBA_BLOCK
    per_b = n // batch
    row = lambda w: pl.BlockSpec((t, w), lambda i: (i, 0))
    full = _const_spec
    out_shape = (
        jax.ShapeDtypeStruct((m, SWA_Q_W), BF16),
        jax.ShapeDtypeStruct((m, 2 * SWA_KV_W), BF16),
        jax.ShapeDtypeStruct((m, 2 * SWA_KV_W), BF16),
        jax.ShapeDtypeStruct((m, MOBA_W), BF16),
        jax.ShapeDtypeStruct((m, MOBA_W), BF16),
        jax.ShapeDtypeStruct((batch, nkb, MOBA_W), F32),
        jax.ShapeDtypeStruct((MOBA_VT_ROWS, m), BF16),
        jax.ShapeDtypeStruct((m, SSD_D_INNER), BF16),
        jax.ShapeDtypeStruct((m, SSD_XBC_W), BF16),
        jax.ShapeDtypeStruct((m, DT_PAD), F32),
        jax.ShapeDtypeStruct((SSD_HEADS, m), F32),
    )
    out_specs = (
        row(SWA_Q_W), row(2 * SWA_KV_W), row(2 * SWA_KV_W), row(MOBA_W), row(MOBA_W),
        pl.BlockSpec((1, nkb, MOBA_W), lambda i: (i // per_b, 0, 0)),
        pl.BlockSpec((MOBA_VT_ROWS, t), lambda i: (0, i)),
        row(SSD_D_INNER), row(SSD_XBC_W), row(DT_PAD),
        pl.BlockSpec((SSD_HEADS, t), lambda i: (0, i)),
    )
    return pl.pallas_call(
        functools.partial(_proj_in_kernel, nkb),
        out_shape=out_shape,
        grid=(n,),
        in_specs=[row(D_MODEL), full(norm_w), full(w_in), full(w_vt), full(w_dtt), row(LANES), row(LANES)],
        out_specs=out_specs,
        compiler_params=_cparams(1),
        name="proj_in",
    )(*_operands(x2d, norm_w, w_in, w_vt, w_dtt, cos_t, sin_t))


def _swa_kernel(sink_ref, q_ref, kp_ref, kc_ref, vp_ref, vc_ref, o_ref):
    i = pl.program_id(1)
    blk = SWA_BLOCK
    kall = jnp.concatenate([kp_ref[...], kc_ref[...]], axis=0)
    vall = jnp.concatenate([vp_ref[...], vc_ref[...]], axis=0)
    qi = lax.broadcasted_iota(jnp.int32, (blk, 2 * blk), 0)
    si = lax.broadcasted_iota(jnp.int32, (blk, 2 * blk), 1)
    delta = qi + blk - si
    in_window = (delta >= 0) & (delta < blk)
    lane = lax.broadcasted_iota(jnp.int32, (blk, LANES), 1)
    low = lane < SWA_HEAD_DIM
    group = SWA_Q_HEADS // SWA_KV_HEADS
    for sub in range(SWA_BLOCKS_PER_STEP):
        rows = slice(sub * blk, (sub + 1) * blk)
        mask = in_window & ((si >= blk) | (i > 0)) if sub == 0 else in_window
        q = q_ref[rows, :] * (SWA_HEAD_DIM ** -0.5)
        kcat = kall[sub * blk:(sub + 2) * blk]
        vcat = vall[sub * blk:(sub + 2) * blk]
        outs = []
        for hd in range(SWA_Q_HEADS):
            g = hd // group
            kd = kcat[:, g * LANES:(g + 1) * LANES]
            vd = vcat[:, g * LANES:(g + 1) * LANES]
            slab = q[:, (hd // 2) * LANES:(hd // 2 + 1) * LANES]
            qm = jnp.where(low if hd % 2 == 0 else ~low, slab, jnp.zeros_like(slab))
            s = jnp.where(mask, _dot_nt(qm, kd), -jnp.inf)
            sink = sink_ref[hd]
            mx = jnp.maximum(jnp.max(s, axis=-1, keepdims=True), sink)
            p = jnp.exp(s - mx)
            denom = jnp.sum(p, axis=-1, keepdims=True) + jnp.exp(sink - mx)
            outs.append(_dot(p.astype(BF16), vd) / denom)
        for pr in range(SWA_Q_HEADS // 2):
            o_ref[rows, pr * LANES:(pr + 1) * LANES] = jnp.where(low, outs[2 * pr], outs[2 * pr + 1]).astype(BF16)


def _swa(q_a, k_a, v_a, sinks, batch):
    m = q_a.shape[0]
    blk = SWA_BLOCK
    per = SWA_BLOCKS_PER_STEP
    nb = m // batch // blk
    steps = nb // per
    kvw = 2 * SWA_KV_W
    cur = lambda b, i: (b * steps + i, 0)
    prev = lambda b, i: (b * nb + jnp.maximum(per * i - 1, 0), 0)
    return pl.pallas_call(
        _swa_kernel,
        out_shape=jax.ShapeDtypeStruct((m, SWA_Q_W), BF16),
        grid=(batch, steps),
        in_specs=[pl.BlockSpec(memory_space=pltpu.SMEM),
                  pl.BlockSpec((per * blk, SWA_Q_W), cur),
                  pl.BlockSpec((blk, kvw), prev), pl.BlockSpec((per * blk, kvw), cur),
                  pl.BlockSpec((blk, kvw), prev), pl.BlockSpec((per * blk, kvw), cur)],
        out_specs=pl.BlockSpec((per * blk, SWA_Q_W), cur),
        compiler_params=_cparams(2),
        name="swa",
    )(sinks, q_a, k_a, k_a, v_a, v_a)


def _fold8(x, op):
    return functools.reduce(op, [x[r:r + 8] for r in range(0, x.shape[0], 8)])


def _moba_kernel(q_ref, k_ref, vt_ref, kmean_ref, o_ref, *scratch):
    i = pl.program_id(2)
    per_pipe = len(scratch) // MOBA_PAIRS_PER_STEP
    pipes = []
    for ps in range(MOBA_PAIRS_PER_STEP):
        lanes = pl.ds(ps * HEAD_PAIR_W, HEAD_PAIR_W)
        vt_rows = pl.ds(ps * 2 * (MOBA_HEAD_DIM + SUM_ROWS), 2 * (MOBA_HEAD_DIM + SUM_ROWS))
        pipes.append(_moba_pipeline(i, q_ref.at[:, lanes], k_ref.at[:, lanes], vt_ref.at[vt_rows, :],
                                    kmean_ref.at[:, :, lanes], o_ref.at[:, lanes],
                                    *scratch[ps * per_pipe:(ps + 1) * per_pipe]))
    n_groups = (i + MOBA_GROUP - 1) // MOBA_GROUP

    for p in pipes:
        p["setup"]()
    for p in pipes:
        p["scores_own"]()
    mx0 = [None] * len(pipes)
    for r in range(0, MOBA_GROUP, MOBA_SCORE_SPAN):
        for n, p in enumerate(pipes):
            mx0[n] = p["score_blocks"](0, 0, r, mx0[n])
            if r == 0:
                p["accumulate_own"]()
    for n, p in enumerate(pipes):
        p["end_scores"](0, mx0[n])

    def stage(score_g, score_buf, acc_g, acc_buf):
        state = [p["begin"](acc_buf) for p in pipes]
        mx = [None] * len(pipes)
        for r in range(0, MOBA_GROUP, MOBA_SCORE_SPAN):
            for n, p in enumerate(pipes):
                mx[n] = p["score_blocks"](score_g, score_buf, r, mx[n])
                for rr in range(r, r + MOBA_SCORE_SPAN):
                    p["exp_block"](acc_g, acc_buf, rr, state[n])
        for n, p in enumerate(pipes):
            p["end_scores"](score_buf, mx[n])

    def pair(u, carry):
        g = 2 * u
        stage(g + 1, 1, g, 0)
        stage(g + 2, 0, g + 1, 1)
        return carry

    lax.fori_loop(0, n_groups // 2, pair, 0)

    @pl.when(n_groups % 2 == 1)
    def _():
        for p in pipes:
            p["accumulate"](n_groups - 1, 0)

    for p in pipes:
        p["finalize"]()


def _moba_pipeline(i, q_ref, k_ref, vt_ref, kmean_ref, o_ref, qm_ref, sel_ref, ml_ref, acc_ref,
                   so_ref, mxo_ref, po_ref, sa_ref, mxa_ref, pa_ref, sb_ref, mxb_ref, pb_ref):
    blk = MOBA_BLOCK
    hd = MOBA_HEAD_DIM
    grp = MOBA_GROUP
    nkb = kmean_ref.shape[1]
    stage_bufs = ((sa_ref, mxa_ref, pa_ref), (sb_ref, mxb_ref, pb_ref))

    def setup():
        qt = (q_ref[...].astype(F32) * (hd ** -0.5)).T
        row = lax.broadcasted_iota(jnp.int32, qt.shape, 0)
        qm_ref[:, 0:blk] = jnp.where(row < hd, qt, 0.0).astype(BF16)
        qm_ref[:, blk:2 * blk] = jnp.where(row >= hd, qt, 0.0).astype(BF16)

        blk_id = lax.broadcasted_iota(jnp.int32, (nkb, 2 * blk), 0)
        gate = None
        resid = kmean_ref[0]
        for _ in range(3):
            part = resid.astype(BF16)
            resid = resid - part.astype(F32)
            term = _dot(part, qm_ref[...])
            gate = term if gate is None else gate + term
        gate = jnp.where(blk_id < i, gate, -jnp.inf)
        sel = jnp.zeros((nkb, 2 * blk), F32)
        for _ in range(min(MOBA_TOPK, nkb)):
            best = jnp.max(gate, axis=0, keepdims=True)
            idx = jnp.min(jnp.where(gate == best, blk_id, nkb), axis=0, keepdims=True)
            pick = blk_id == idx
            sel = jnp.where(pick & (blk_id < i), 1.0, sel)
            gate = jnp.where(pick, -jnp.inf, gate)
        sel_ref[0:nkb, :] = sel
        sel_ref[nkb:nkb + grp, :] = jnp.zeros((grp, 2 * blk), F32)

    def score_blocks(start, r, keeps, s_buf, mx):
        n = len(keeps)
        s_all = _dot(k_ref[pl.ds(start + r * blk, n * blk), :], qm_ref[...])
        for j, keep in enumerate(keeps):
            s = jnp.where(keep, s_all[j * blk:(j + 1) * blk], -jnp.inf)
            s_buf[(r + j) * blk:(r + j + 1) * blk, :] = s
            m8 = _fold8(s, jnp.maximum)
            mx = m8 if mx is None else jnp.maximum(mx, m8)
        return mx

    def begin(mx_buf, first):
        mx = jnp.max(mx_buf[...], axis=0, keepdims=True)
        if first:
            m_new, alpha = mx, None
        else:
            m_old = ml_ref[0:1, :]
            m_new = jnp.maximum(m_old, mx)
            alpha = jnp.exp2(m_old - m_new)
            for h in range(2):
                acc_ref[h] = alpha[:, h * blk:(h + 1) * blk] * acc_ref[h]
        ml_ref[0:1, :] = m_new
        return m_new, alpha

    def exp_block(s_buf, p_buf, r, m_new):
        p_buf[r * blk:(r + 1) * blk, :] = jnp.exp2(s_buf[r * blk:(r + 1) * blk, :] - m_new).astype(BF16)

    def pv_block(start, r, p_buf, assign):
        ext = hd + SUM_ROWS
        for h in range(2):
            vt_ext = vt_ref[h * ext:(h + 1) * ext, pl.ds(start + r * blk, blk)]
            term = _dot(vt_ext, p_buf[r * blk:(r + 1) * blk, h * blk:(h + 1) * blk])
            acc_ref[h] = term if assign else acc_ref[h] + term

    def group_start(g):
        return pl.multiple_of(jnp.minimum(g, nkb // grp - 1) * (grp * blk), grp * blk)

    def group_score_blocks(g, buf, r, mx):
        keeps = [sel_ref[pl.ds(g * grp + r + j, 1), :] > 0.0 for j in range(MOBA_SCORE_SPAN)]
        return score_blocks(group_start(g), r, keeps, stage_bufs[buf][0], mx)

    def group_end_scores(buf, mx):
        stage_bufs[buf][1][...] = mx

    def group_begin(buf):
        return begin(stage_bufs[buf][1], False)

    def group_exp_block(g, buf, r, state):
        exp_block(stage_bufs[buf][0], stage_bufs[buf][2], r, state[0])
        pv_block(group_start(g), r, stage_bufs[buf][2], False)

    def group_accumulate(g, buf):
        state = group_begin(buf)
        for r in range(grp):
            group_exp_block(g, buf, r, state)

    own = pl.multiple_of(i * blk, blk)

    def scores_own():
        kpos = lax.broadcasted_iota(jnp.int32, (blk, 2 * blk), 0)
        qpos = lax.broadcasted_iota(jnp.int32, (blk, 2 * blk), 1) % blk
        mxo_ref[...] = score_blocks(own, 0, [kpos <= qpos], so_ref, None)

    def accumulate_own():
        m_new, _ = begin(mxo_ref, True)
        exp_block(so_ref, po_ref, 0, m_new)
        pv_block(own, 0, po_ref, True)

    def finalize():
        out_t = jnp.concatenate([acc_ref[h, 0:hd, :] / acc_ref[h, hd:hd + 1, :] for h in range(2)], axis=0)
        o_ref[...] = out_t.T.astype(BF16)

    return dict(setup=setup, scores_own=scores_own, accumulate_own=accumulate_own,
                accumulate=group_accumulate, finalize=finalize, begin=group_begin,
                score_blocks=group_score_blocks, exp_block=group_exp_block, end_scores=group_end_scores)


def _moba_stage(n_blocks, blk):
    return [pltpu.VMEM((n_blocks * blk, 2 * blk), F32), pltpu.VMEM((8, 2 * blk), F32),
            pltpu.VMEM((n_blocks * blk, 2 * blk), BF16)]


def _moba(q_b, k_b, vt_b, kmean, batch):
    m = q_b.shape[0]
    s = m // batch
    blk = MOBA_BLOCK
    nkb = s // blk
    assert nkb % MOBA_GROUP == 0
    w = MOBA_PAIRS_PER_STEP * HEAD_PAIR_W
    steps = MOBA_W // w
    pipe_scratch = [pltpu.VMEM((HEAD_PAIR_W, 2 * blk), BF16),
                    pltpu.VMEM((nkb + MOBA_GROUP, 2 * blk), F32),
                    pltpu.VMEM((8, 2 * blk), F32),
                    pltpu.VMEM((2, MOBA_HEAD_DIM + SUM_ROWS, blk), F32),
                    ] + _moba_stage(1, blk) + _moba_stage(MOBA_GROUP, blk) + _moba_stage(MOBA_GROUP, blk)
    return pl.pallas_call(
        _moba_kernel,
        out_shape=jax.ShapeDtypeStruct((m, MOBA_W), BF16),
        grid=(batch, steps, nkb),
        in_specs=[pl.BlockSpec((blk, w), lambda b, p, i: (b * nkb + i, p)),
                  pl.BlockSpec((s, w), lambda b, p, i: (b, p), pipeline_mode=pl.Buffered(1)),
                  pl.BlockSpec((MOBA_VT_ROWS // steps, s), lambda b, p, i: (p, b), pipeline_mode=pl.Buffered(1)),
                  pl.BlockSpec((1, nkb, w), lambda b, p, i: (b, 0, p))],
        out_specs=pl.BlockSpec((blk, w), lambda b, p, i: (b * nkb + i, p)),
        scratch_shapes=pipe_scratch * MOBA_PAIRS_PER_STEP,
        compiler_params=_cparams(3),
        name="moba",
    )(q_b, k_b, vt_b, kmean)


def _split_dot(v, e):
    hi = v.astype(BF16)
    lo = (v - hi.astype(F32)).astype(BF16)
    return _dot(hi, e) + _dot(lo, e)


def _ssd_kernel(z_ref, xbc_ref, dt_ref, dtt_ref, cw_ref, cb_ref, dtb_ref, dtbc_ref, alog_ref, alogc_ref,
                dskip_ref, nw_ref, exp_ref, shift_ref, o_ref, ext_ref, state_ref):
    c = pl.program_id(1)
    L = SSD_CHUNK
    tail = CONV_TAIL

    last = SSD_CHUNKS_PER_STEP * L

    @pl.when(c > 0)
    def _():
        ext_ref[0:tail, :] = ext_ref[last:last + tail, :]

    @pl.when(c == 0)
    def _():
        ext_ref[0:tail, :] = jnp.zeros((tail, SSD_XBC_W), BF16)
        state_ref[...] = jnp.zeros_like(state_ref)

    ext_ref[tail:tail + last, :] = xbc_ref[...]
    for sub in range(SSD_CHUNKS_PER_STEP):
        rows = slice(sub * L, (sub + 1) * L)
        window = ext_ref[sub * L:(sub + 1) * L + tail, :]
        _ssd_chunk(window, xbc_ref[rows, :], z_ref[rows, :], dt_ref[rows, :], dtt_ref[:, rows], cw_ref, cb_ref,
                   dtb_ref, dtbc_ref, alog_ref, alogc_ref, dskip_ref, nw_ref, exp_ref, shift_ref,
                   o_ref.at[rows, :], state_ref)


def _ssd_chunk(window, xbc, z, dt_raw, dtt_raw, cw_ref, cb_ref, dtb_ref, dtbc_ref, alog_ref, alogc_ref,
               dskip_ref, nw_ref, exp_ref, shift_ref, o_ref, state_ref):
    L = SSD_CHUNK
    shifted = _dot(shift_ref[...], window)
    acc = cb_ref[...] + cw_ref[SSD_CONV - 1:SSD_CONV, :] * xbc.astype(F32)
    for j in range(SSD_CONV - 1):
        acc = acc + cw_ref[j:j + 1, :] * shifted[j * L:(j + 1) * L, :]
    xc = _silu(acc)
    xs = xc[:, :SSD_D_INNER]
    bm = xc[:, SSD_D_INNER:SSD_D_INNER + SSD_BC_W]
    cm = xc[:, SSD_D_INNER + SSD_BC_W:]

    dt = jax.nn.softplus(dt_raw + dtb_ref[...])
    dtt = jax.nn.softplus(dtt_raw + dtbc_ref[...])
    a_row = -jnp.exp(alog_ref[...]) * LOG2E
    a_col = -jnp.exp(alogc_ref[...]) * LOG2E
    r_i = lax.broadcasted_iota(jnp.int32, (L, L), 0)
    c_i = lax.broadcasted_iota(jnp.int32, (L, L), 1)
    tril = r_i >= c_i
    lower = tril.astype(BF16)
    upper = (r_i <= c_i).astype(BF16)
    a_cs, a_cst = None, None
    res, rest = dt * a_row, dtt * a_col
    for _ in range(3):
        part, partt = res.astype(BF16), rest.astype(BF16)
        res, rest = res - part.astype(F32), rest - partt.astype(F32)
        term, termt = _dot(lower, part), _dot(partt, upper)
        a_cs = term if a_cs is None else a_cs + term
        a_cst = termt if a_cst is None else a_cst + termt
    a_src = a_cst - jnp.log2(dtt)
    a_last = a_cs[L - 1:L, :]
    expand = exp_ref[...]
    decay_out = _split_dot(jnp.exp2(a_cs), expand)
    decay_in = _split_dot(dt * jnp.exp2(a_last - a_cs), expand)
    chunk_decay = decay_out[L - 1:L, :]

    lane = lax.broadcasted_iota(jnp.int32, (L, LANES), 1)
    low = lane < SSD_HEAD_DIM
    heads_per_group = SSD_HEADS // SSD_GROUPS
    ys = []
    for g in range(SSD_GROUPS):
        b_g = bm[:, g * SSD_D_STATE:(g + 1) * SSD_D_STATE]
        c_g = cm[:, g * SSD_D_STATE:(g + 1) * SSD_D_STATE]
        cb = _dot_nt(c_g.astype(BF16), b_g.astype(BF16))
        gcols = slice(g * SSD_GROUP_W, (g + 1) * SSD_GROUP_W)
        y_off = _dot(c_g.astype(BF16), state_ref[g].astype(BF16)) * decay_out[:, gcols]
        y_diag = []
        for pr in range(heads_per_group // 2):
            col0 = g * SSD_GROUP_W + pr * LANES
            x_pair = xs[:, col0:col0 + LANES].astype(BF16)
            halves = []
            for e in range(2):
                hidx = g * heads_per_group + 2 * pr + e
                diff = a_cs[:, hidx:hidx + 1] - a_src[hidx:hidx + 1, :]
                wmat = cb * jnp.exp2(jnp.where(tril, diff, -jnp.inf))
                halves.append(_dot(wmat.astype(BF16), x_pair))
            y_diag.append(jnp.where(low, halves[0], halves[1]))
        ys.append(jnp.concatenate(y_diag, axis=1) + y_off)
        xw = (xs[:, gcols] * decay_in[:, gcols]).astype(BF16)
        state_ref[g] = state_ref[g] * chunk_decay[:, gcols] + _dot(b_g.T.astype(BF16), xw)

    y = jnp.concatenate(ys, axis=1) + xs * dskip_ref[...]
    y = y * _silu(z.astype(F32))
    outs = []
    for g in range(SSD_GROUPS):
        yg = y[:, g * SSD_GROUP_W:(g + 1) * SSD_GROUP_W]
        outs.append(yg * lax.rsqrt(jnp.mean(yg * yg, axis=-1, keepdims=True) + NORM_EPS))
    o_ref[...] = (jnp.concatenate(outs, axis=1) * nw_ref[...]).astype(BF16)


def _ssd(z, xbc, dt, dtt, sp, batch):
    m = z.shape[0]
    L = SSD_CHUNKS_PER_STEP * SSD_CHUNK
    nc = m // batch // L
    row = lambda w: pl.BlockSpec((L, w), lambda b, c: (b * nc + c, 0))
    full = _const_spec
    consts = (sp["conv_w"], sp["conv_b"], sp["dt_bias_row"], sp["dt_bias_col"], sp["a_log_row"],
              sp["a_log_col"], sp["d_skip"], sp["norm_w"], sp["expand"], sp["shift"])
    return pl.pallas_call(
        _ssd_kernel,
        out_shape=jax.ShapeDtypeStruct((m, SSD_D_INNER), BF16),
        grid=(batch, nc),
        in_specs=[row(SSD_D_INNER), row(SSD_XBC_W), row(DT_PAD),
                  pl.BlockSpec((SSD_HEADS, L), lambda b, c: (0, b * nc + c))] + [full(a) for a in consts],
        out_specs=row(SSD_D_INNER),
        scratch_shapes=[pltpu.VMEM((L + CONV_TAIL, SSD_XBC_W), BF16),
                        pltpu.VMEM((SSD_GROUPS, SSD_D_STATE, SSD_GROUP_W), F32)],
        compiler_params=_cparams(2),
        name="ssd",
    )(*_operands(z, xbc, dt, dtt, *consts))


def _mem_kv_kernel(mem_ref, nw_ref, wkt_ref, w_ref, kt_ref, v_ref):
    hm = _rms(mem_ref[...], nw_ref[...])
    kt_ref[...] = _dot(wkt_ref[...], hm.T.astype(BF16)).astype(BF16)
    v_ref[...] = _dot(hm.astype(BF16), w_ref[:, D_MODEL:]).astype(BF16)


def _mem_kv(mem2d, norm_w, w_ckt, w_ckv):
    m = mem2d.shape[0]
    t = MEM_LEN
    row = pl.BlockSpec((t, D_MODEL), lambda i: (i, 0))
    full = _const_spec
    return pl.pallas_call(
        _mem_kv_kernel,
        out_shape=(jax.ShapeDtypeStruct((m // t * D_MODEL, t), BF16), jax.ShapeDtypeStruct((m, D_MODEL), BF16)),
        grid=(m // t,),
        in_specs=[row, full(norm_w), full(w_ckt), full(w_ckv)],
        out_specs=(pl.BlockSpec((D_MODEL, t), lambda i: (i, 0)), row),
        compiler_params=_cparams(1),
        name="mem_kv",
    )(*_operands(mem2d, norm_w, w_ckt, w_ckv))


def _merge_kernel(final, x_ref, ya_ref, yb_ref, yc_ref, nmix_ref, wg_ref, bg_ref, wa_ref, wb_ref, wc_ref, wmix_ref,
                  ncross_ref, wcq_ref, ktmem_ref, vmem_ref, wco_ref, nmlp_ref, wup_ref, wdown_ref, fnw_ref, o_ref):
    x = x_ref[...]
    h = _rms(x, nmix_ref[...]).astype(BF16)
    merged = None
    for br, (y_ref, w_ref) in enumerate(((ya_ref, wa_ref), (yb_ref, wb_ref), (yc_ref, wc_ref))):
        cols = slice(br * D_MODEL, (br + 1) * D_MODEL)
        gate = jax.nn.sigmoid(_dot(h, wg_ref[:, cols]) + bg_ref[:, cols])
        term = gate * _dot(y_ref[...], w_ref[...])
        merged = term if merged is None else merged + term
    x = x + _dot(merged.astype(BF16), wmix_ref[...])

    hq = _rms(x, ncross_ref[...]).astype(BF16)
    q = _dot(hq, wcq_ref[...]).astype(BF16)
    scale = X_HEAD_DIM ** -0.5
    outs = []
    for hh in range(X_HEADS):
        cols = slice(hh * X_HEAD_DIM, (hh + 1) * X_HEAD_DIM)
        s = _dot(q[:, cols], ktmem_ref[cols, :]) * scale
        p = jnp.exp(s - jnp.max(s, axis=-1, keepdims=True))
        denom = jnp.sum(p, axis=-1, keepdims=True)
        outs.append(_dot(p.astype(BF16), vmem_ref[:, cols]) / denom)
    o = jnp.concatenate(outs, axis=1).astype(BF16)
    x = x + _dot(o, wco_ref[...])

    hf = _rms(x, nmlp_ref[...]).astype(BF16)
    chunk = D_MODEL
    for c in range(D_FF // chunk):
        cols = slice(c * chunk, (c + 1) * chunk)
        u = jnp.maximum(_dot(hf, wup_ref[:, cols]), 0.0)
        x = x + _dot((u * u).astype(BF16), wdown_ref[cols, :])
    if final:
        x = _rms(x, fnw_ref[...])
    o_ref[...] = x


def _merge(x2d, y_a, y_b, y_c, ktmem, vmem, lp, fnw, batch, final):
    m = x2d.shape[0]
    t = MLP_ROW_TILE
    n = m // t
    per_b = n // batch
    row = lambda w: pl.BlockSpec((t, w), lambda i: (i, 0))
    full = _const_spec
    memspec = pl.BlockSpec((MEM_LEN, D_MODEL), lambda i: (i // per_b, 0))
    ktspec = pl.BlockSpec((D_MODEL, MEM_LEN), lambda i: (i // per_b, 0))
    consts_a = (lp["norm_mix"], lp["w_gate"], lp["b_gate"], lp["w_br_swa"], lp["w_br_moba"], lp["w_br_ssd"],
                lp["w_mix_out"], lp["norm_cross"], lp["w_cq"])
    consts_b = (lp["w_co"], lp["norm_mlp"], lp["w_up"], lp["w_down"], fnw)
    return pl.pallas_call(
        functools.partial(_merge_kernel, final),
        out_shape=jax.ShapeDtypeStruct((m, D_MODEL), F32),
        grid=(n,),
        in_specs=[row(D_MODEL), row(SWA_Q_W), row(MOBA_W), row(SSD_D_INNER)] + [full(a) for a in consts_a]
                 + [ktspec, memspec] + [full(a) for a in consts_b],
        out_specs=row(D_MODEL),
        compiler_params=_cparams(1),
        name="merge_mlp_final" if final else "merge_mlp",
    )(*_operands(x2d, y_a, y_b, y_c, *consts_a, ktmem, vmem, *consts_b))


_MATMUL_WEIGHTS = ("w_gate", "w_br_swa", "w_br_moba", "w_br_ssd", "w_mix_out", "w_cq", "w_ckv", "w_co",
                   "w_up", "w_down")
_ROW_PARAMS = ("norm_mix", "b_gate", "norm_cross", "norm_mem", "norm_mlp", "conv_b", "ssd_norm")


def _prep_params(p):
    depth = p["w_in"].shape[0]
    out = {k: p[k].astype(BF16) for k in _MATMUL_WEIGHTS}
    out.update({k: p[k].astype(F32).reshape(depth, 1, -1) for k in _ROW_PARAMS})
    out["w_in"] = jnp.pad(p["w_in"].astype(BF16), ((0, 0), (0, 0), (0, DT_PAD - SSD_HEADS)))
    out["w_ckt"] = jnp.swapaxes(out["w_ckv"][:, :, :D_MODEL], 1, 2)
    out["w_vt"] = jnp.swapaxes(out["w_in"][:, :, _C_VB:_C_Z], 1, 2)
    out["w_dtt"] = jnp.swapaxes(out["w_in"][:, :, _C_DT:D_IN_PROJ], 1, 2)
    pad_row = lambda v: jnp.pad(v.astype(F32), ((0, 0), (0, DT_PAD - SSD_HEADS))).reshape(depth, 1, DT_PAD)
    col = lambda v: v.astype(F32).reshape(depth, SSD_HEADS, 1)
    out.update(conv_w=p["conv_w"].astype(F32), dt_bias_row=pad_row(p["dt_bias"]), dt_bias_col=col(p["dt_bias"]),
               a_log_row=pad_row(p["a_log"]), a_log_col=col(p["a_log"]),
               d_skip=jnp.repeat(p["d_skip"].astype(F32), SSD_HEAD_DIM, axis=1).reshape(depth, 1, SSD_D_INNER),
               swa_sinks=p["swa_sinks"].astype(F32))
    return out


def _ssd_constants():
    head_of_col = jnp.arange(SSD_D_INNER) // SSD_HEAD_DIM
    expand = (jnp.arange(DT_PAD)[:, None] == head_of_col[None, :]).astype(BF16)
    t_idx = jnp.arange(SSD_CHUNK)[None, :, None]
    j_idx = jnp.arange(SSD_CONV - 1)[:, None, None]
    col = jnp.arange(SSD_CHUNK + CONV_TAIL)[None, None, :]
    shift = (col == CONV_TAIL + t_idx - (SSD_CONV - 1) + j_idx).astype(BF16)
    return expand, shift.reshape((SSD_CONV - 1) * SSD_CHUNK, SSD_CHUNK + CONV_TAIL)


def _layer_params(l, sp, expand, shift):
    lp = {k: _LayerParam(v, l) for k, v in sp.items() if k != "swa_sinks"}
    lp["swa_sinks"] = sp["swa_sinks"][l]
    lp["ssd"] = {"conv_w": lp["conv_w"], "conv_b": lp["conv_b"], "dt_bias_row": lp["dt_bias_row"],
                 "dt_bias_col": lp["dt_bias_col"], "a_log_row": lp["a_log_row"], "a_log_col": lp["a_log_col"],
                 "d_skip": lp["d_skip"], "norm_w": lp["ssd_norm"], "expand": expand, "shift": shift}
    return lp


def kernel(x, mem, positions, norm_mix, w_in, w_gate, b_gate, swa_sinks, conv_w, conv_b, dt_bias, a_log, d_skip,
           ssd_norm, w_br_swa, w_br_moba, w_br_ssd, w_mix_out, norm_cross, norm_mem, w_cq, w_ckv, w_co, norm_mlp,
           w_up, w_down, final_norm):
    batch, seq, d = x.shape
    depth = w_in.shape[0]
    assert d == D_MODEL and seq % MOBA_BLOCK == 0 and mem.shape[1] == MEM_LEN
    params = dict(norm_mix=norm_mix, w_in=w_in, w_gate=w_gate, b_gate=b_gate, swa_sinks=swa_sinks, conv_w=conv_w,
                  conv_b=conv_b, dt_bias=dt_bias, a_log=a_log, d_skip=d_skip, ssd_norm=ssd_norm, w_br_swa=w_br_swa,
                  w_br_moba=w_br_moba, w_br_ssd=w_br_ssd, w_mix_out=w_mix_out, norm_cross=norm_cross,
                  norm_mem=norm_mem, w_cq=w_cq, w_ckv=w_ckv, w_co=w_co, norm_mlp=norm_mlp, w_up=w_up, w_down=w_down)
    x2d = x.reshape(batch * seq, d)
    mem2d = mem.reshape(batch * MEM_LEN, d)
    fnw = final_norm.reshape(1, d).astype(F32)
    cos_t, sin_t = _rope_tables(positions)
    stacked = _prep_params(params)
    expand, shift = _ssd_constants()
    for l in range(depth):
        lp = _layer_params(l, stacked, expand, shift)
        (q_a, k_a, v_a, q_b, k_b, kmean, vt_b, z, xbc, dt, dtt) = _proj_in(
            x2d, lp["norm_mix"], lp["w_in"], lp["w_vt"], lp["w_dtt"], cos_t, sin_t, batch)
        y_a = _swa(q_a, k_a, v_a, lp["swa_sinks"], batch)
        y_b = _moba(q_b, k_b, vt_b, kmean, batch)
        y_c = _ssd(z, xbc, dt, dtt, lp["ssd"], batch)
        ktmem, vmem = _mem_kv(mem2d, lp["norm_mem"], lp["w_ckt"], lp["w_ckv"])
        x2d = _merge(x2d, y_a, y_b, y_c, ktmem, vmem, lp, fnw, batch, final=(l == depth - 1))
    return x2d.reshape(batch, seq, d)
```
